```python
import jax, jax.numpy as jnp
from jax import lax
import numpy as np

D_MODEL = 2048
BATCH = 8
SEQ = 2048
DEPTH = 1

GM_WIDTH = D_MODEL
GM_GROUPS = 16
GM_GROUP_DIM = GM_WIDTH // GM_GROUPS
GM_CHUNK = 128

NSA_HEADS = 16
NSA_KV_HEADS = 4
NSA_GROUP = NSA_HEADS // NSA_KV_HEADS
HEAD_DIM = D_MODEL // NSA_HEADS
NSA_BRANCHES = 3
CMP_BLOCK = 32
CMP_STRIDE = 16
CMP_HIDDEN = 256
SLC_BLOCK = 64
N_SELECT = 16
WINDOW = 512
WIN_BLOCK = 128
SLC_QUERY_CHUNK = 16
ROPE_THETA = 10000.0

PEER_HEADS = 8
PEER_QUERY_DIM = 256
PEER_N_KEYS = 128
PEER_N_EXPERTS = PEER_N_KEYS * PEER_N_KEYS
PEER_TOPK = 16
PEER_TOKEN_BLOCK = 128

EPS = 1e-6
NEG_BIG = -1e30
POS_BIG = 1e30

IN_SIZES = (GM_WIDTH, GM_WIDTH, NSA_HEADS * HEAD_DIM, 6 * NSA_KV_HEADS * HEAD_DIM, NSA_HEADS * NSA_BRANCHES, 2 * D_MODEL)
IN_TOTAL = 2 * GM_WIDTH + NSA_HEADS * HEAD_DIM + 6 * NSA_KV_HEADS * HEAD_DIM + NSA_HEADS * NSA_BRANCHES + 2 * D_MODEL

kernel_name = 'hybrid_gmlp_nsa_peer_block'


def rms_norm(x, g):
    xf = x.astype(jnp.float32)
    y = xf * lax.rsqrt(jnp.mean(xf * xf, axis=-1, keepdims=True) + EPS)
    return (y * g.astype(jnp.float32)).astype(x.dtype)


def layer_norm(x, g, b):
    xf = x.astype(jnp.float32)
    mu = jnp.mean(xf, axis=-1, keepdims=True)
    xc = xf - mu
    y = xc * lax.rsqrt(jnp.mean(xc * xc, axis=-1, keepdims=True) + EPS)
    return (y * g.astype(jnp.float32) + b.astype(jnp.float32)).astype(x.dtype)


def masked_softmax(s, mask, axis=-1):
    s = jnp.where(mask, s.astype(jnp.float32), NEG_BIG)
    p = jax.nn.softmax(s, axis=axis)
    return jnp.where(jnp.any(mask, axis=axis, keepdims=True), p, 0.0)


def rotary(x, pos):
    half = HEAD_DIM // 2
    inv = ROPE_THETA ** (-jnp.arange(half, dtype=jnp.float32) / half)
    ang = pos.astype(jnp.float32)[:, None] * inv[None, :]
    shp = (1, x.shape[1]) + (1,) * (x.ndim - 3) + (half,)
    cos = jnp.cos(ang).reshape(shp)
    sin = jnp.sin(ang).reshape(shp)
    xf = x.astype(jnp.float32)
    x1, x2 = xf[..., :half], xf[..., half:]
    return jnp.concatenate([x1 * cos - x2 * sin, x2 * cos + x1 * sin], axis=-1).astype(x.dtype)


def gmlp_branch(u, v, ln_g, ln_b, w_s, b_s):
    B, S, _ = u.shape
    u = jax.nn.gelu(u)
    v = layer_norm(jax.nn.gelu(v), ln_g, ln_b)
    v = v.reshape(B, S // GM_CHUNK, GM_CHUNK, GM_GROUPS, GM_GROUP_DIM)
    causal = jnp.tril(jnp.ones((GM_CHUNK, GM_CHUNK), dtype=bool))
    w = jnp.where(causal[None], w_s, 0.0)
    z = jnp.einsum('gts,bcsgd->bctgd', w, v) + b_s.T[None, None, :, :, None]
    return u * z.reshape(B, S, GM_WIDTH)


def compress_blocks(k, pos_emb, w1, w2):
    B, S = k.shape[0], k.shape[1]
    n_cmp = (S - CMP_BLOCK) // CMP_STRIDE + 1
    idx = jnp.arange(n_cmp)[:, None] * CMP_STRIDE + jnp.arange(CMP_BLOCK)[None, :]
    blocks = k[:, idx] + pos_emb[None, None, :, None, :]
    flat = jnp.moveaxis(blocks, 3, 2).reshape(B, n_cmp, NSA_KV_HEADS, CMP_BLOCK * HEAD_DIM)
    return jax.nn.gelu(flat @ w1) @ w2


def select_blocks(p_cmp, pos):
    B, S = p_cmp.shape[0], p_cmp.shape[3]
    n_cmp = p_cmp.shape[4]
    n_slc = S // SLC_BLOCK
    ratio = SLC_BLOCK // CMP_STRIDE
    span = CMP_BLOCK // CMP_STRIDE
    imp = jnp.sum(p_cmp, axis=2)
    right = max(0, ratio * n_slc - n_cmp)
    imp_pad = jnp.pad(imp, ((0, 0), (0, 0), (0, 0), (span - 1, right)))
    agg_w = jnp.asarray(np.convolve(np.ones(ratio), np.ones(span)), dtype=jnp.float32)
    gidx = ratio * jnp.arange(n_slc)[:, None] + jnp.arange(ratio + span - 1)[None, :]
    imp_slc = jnp.einsum('bhsjw,w->bhsj', imp_pad[..., gidx], agg_w)
    blk = jnp.arange(n_slc)
    allowed = (blk * SLC_BLOCK)[None, :] <= pos[:, None]
    forced = (blk[None, :] == 0) | (blk[None, :] == (pos // SLC_BLOCK)[:, None])
    score = jnp.where(forced, POS_BIG, jnp.where(allowed, imp_slc, NEG_BIG))
    _, sel = lax.top_k(score, min(N_SELECT, n_slc))
    return sel


def selected_attention(q_rot, k, v, sel, pos):
    B, S = q_rot.shape[0], q_rot.shape[1]
    n_slc = S // SLC_BLOCK
    n_sel = sel.shape[-1]
    n_chunks = S // SLC_QUERY_CHUNK
    scale = HEAD_DIM ** -0.5
    k_blk = k.reshape(B, n_slc, SLC_BLOCK, NSA_KV_HEADS, HEAD_DIM).transpose(0, 3, 1, 2, 4)
    v_blk = v.reshape(B, n_slc, SLC_BLOCK, NSA_KV_HEADS, HEAD_DIM).transpose(0, 3, 1, 2, 4)
    q_c = q_rot.reshape(B, n_chunks, SLC_QUERY_CHUNK, NSA_KV_HEADS, NSA_GROUP, HEAD_DIM).swapaxes(0, 1)
    sel_c = sel.transpose(0, 2, 1, 3).reshape(B, n_chunks, SLC_QUERY_CHUNK, NSA_KV_HEADS, n_sel).swapaxes(0, 1)
    pos_c = pos.reshape(n_chunks, SLC_QUERY_CHUNK)
    b_ar = jnp.arange(B)[:, None, None, None]
    h_ar = jnp.arange(NSA_KV_HEADS)[None, None, :, None]

    def attend(args):
        qc, ic, pc = args
        kg = k_blk[b_ar, h_ar, ic]
        vg = v_blk[b_ar, h_ar, ic]
        s = jnp.einsum('bthgd,bthnkd->bthgnk', qc, kg) * scale
        tokpos = ic[..., None] * SLC_BLOCK + jnp.arange(SLC_BLOCK)
        mask = (tokpos <= pc[None, :, None, None, None])[:, :, :, None]
        Bq, Tq = qc.shape[0], qc.shape[1]
        p = masked_softmax(s.reshape(Bq, Tq, NSA_KV_HEADS, NSA_GROUP, n_sel * SLC_BLOCK),
                           mask.reshape(Bq, Tq, NSA_KV_HEADS, 1, n_sel * SLC_BLOCK))
        p = p.reshape(s.shape).astype(vg.dtype)
        return jnp.einsum('bthgnk,bthnkd->bthgd', p, vg)

    o = lax.map(attend, (q_c, sel_c, pos_c))
    return o.swapaxes(0, 1).reshape(B, S, NSA_KV_HEADS, NSA_GROUP, HEAD_DIM)


def window_attention(q_rot, k, v):
    B, S = q_rot.shape[0], q_rot.shape[1]
    nb = S // WIN_BLOCK
    n_prev = WINDOW // WIN_BLOCK
    scale = HEAD_DIM ** -0.5

    def band(t):
        t_pad = jnp.pad(t, ((0, 0), (WINDOW, 0), (0, 0), (0, 0)))
        t_pad = t_pad.reshape(B, nb + n_prev, WIN_BLOCK, NSA_KV_HEADS, HEAD_DIM)
        return jnp.concatenate([t_pad[:, i:i + nb] for i in range(n_prev + 1)], axis=2)

    kb, vb = band(k), band(v)
    qb = q_rot.reshape(B, nb, WIN_BLOCK, NSA_KV_HEADS, NSA_GROUP, HEAD_DIM)
    s = jnp.einsum('bcqhgd,bckhd->bchgqk', qb, kb) * scale
    qpos = jnp.arange(S).reshape(nb, WIN_BLOCK)
    kpos = jnp.arange(nb)[:, None] * WIN_BLOCK - WINDOW + jnp.arange((n_prev + 1) * WIN_BLOCK)[None, :]
    diff = qpos[:, :, None] - kpos[:, None, :]
    mask = (diff >= 0) & (diff < WINDOW) & (kpos[:, None, :] >= 0)
    p = masked_softmax(s, mask[None, :, None, None]).astype(vb.dtype)
    o = jnp.einsum('bchgqk,bckhd->bcqhgd', p, vb)
    return o.reshape(B, S, NSA_KV_HEADS, NSA_GROUP, HEAD_DIM)


def nsa_branch(q, kv, gate_logits, cmp_pos_k, cmp_w1_k, cmp_w2_k, cmp_pos_v, cmp_w1_v, cmp_w2_v):
    B, S, _ = q.shape
    pos = jnp.arange(S)
    scale = HEAD_DIM ** -0.5
    q = q.reshape(B, S, NSA_KV_HEADS, NSA_GROUP, HEAD_DIM)
    kv = kv.reshape(B, S, 6, NSA_KV_HEADS, HEAD_DIM)
    k_cmp, v_cmp, k_slc, v_slc, k_win, v_win = (kv[:, :, i] for i in range(6))
    q_rot = rotary(q, pos)
    k_slc = rotary(k_slc, pos)
    k_win = rotary(k_win, pos)

    kc = compress_blocks(k_cmp, cmp_pos_k, cmp_w1_k, cmp_w2_k)
    vc = compress_blocks(v_cmp, cmp_pos_v, cmp_w1_v, cmp_w2_v)
    n_cmp = kc.shape[1]
    s = jnp.einsum('bshgd,bnhd->bhgsn', q, kc) * scale
    cmp_end = jnp.arange(n_cmp) * CMP_STRIDE + CMP_BLOCK - 1
    p_cmp = masked_softmax(s, cmp_end[None, :] <= pos[:, None])
    o_cmp = jnp.einsum('bhgsn,bnhd->bshgd', p_cmp.astype(vc.dtype), vc)

    sel = select_blocks(p_cmp, pos)
    o_slc = selected_attention(q_rot, k_slc, v_slc, sel, pos)

    o_win = window_attention(q_rot, k_win, v_win)

    g = jax.nn.sigmoid(gate_logits).reshape(B, S, NSA_KV_HEADS, NSA_GROUP, NSA_BRANCHES)
    o = g[..., 0:1] * o_cmp + g[..., 1:2] * o_slc + g[..., 2:3] * o_win
    return o.reshape(B, S, NSA_HEADS * HEAD_DIM)


def peer(h, w_q, keys1, keys2, u_tab, v_tab):
    B, S, D = h.shape
    T = B * S
    ht = h.reshape(T, D)
    q = (ht @ w_q).reshape(T, PEER_HEADS, 2, PEER_QUERY_DIM // 2)
    s1 = jnp.einsum('thd,kd->thk', q[:, :, 0], keys1)
    s2 = jnp.einsum('thd,kd->thk', q[:, :, 1], keys2)
    v1, i1 = lax.top_k(s1, PEER_TOPK)
    v2, i2 = lax.top_k(s2, PEER_TOPK)
    cand = (v1[..., :, None] + v2[..., None, :]).reshape(T, PEER_HEADS, PEER_TOPK * PEER_TOPK)
    cand_id = (i1[..., :, None] * PEER_N_KEYS + i2[..., None, :]).reshape(T, PEER_HEADS, PEER_TOPK * PEER_TOPK)
    top_s, top_pos = lax.top_k(cand, PEER_TOPK)
    experts = jnp.take_along_axis(cand_id, top_pos, axis=-1)
    gate = jax.nn.softmax(top_s.astype(jnp.float32), axis=-1).astype(h.dtype)
    nblk = T // PEER_TOKEN_BLOCK

    def apply(args):
        xb, eb, gb = args
        ue = u_tab[eb]
        act = jax.nn.gelu(jnp.einsum('td,thkd->thk', xb, ue))
        ve = v_tab[eb]
        return jnp.einsum('thk,thkd->td', gb * act, ve)

    out = lax.map(apply, (ht.reshape(nblk, PEER_TOKEN_BLOCK, D),
                          experts.reshape(nblk, PEER_TOKEN_BLOCK, PEER_HEADS, PEER_TOPK),
                          gate.reshape(nblk, PEER_TOKEN_BLOCK, PEER_HEADS, PEER_TOPK)))
    return out.reshape(B, S, D)


def setup_inputs(seed: int = 0) -> dict:
    key = jax.random.key(seed)
    ks = jax.random.split(key, 24)
    f32 = jnp.float32
    L = DEPTH

    def nrm(k, shape, scale):
        return jax.random.normal(k, shape, dtype=f32) * scale

    return {
        'x': nrm(ks[0], (BATCH, SEQ, D_MODEL), 1.0),
        'norm_mix_g': 1.0 + nrm(ks[1], (L, D_MODEL), 0.01),
        'w_in': nrm(ks[2], (L, D_MODEL, IN_TOTAL), D_MODEL ** -0.5),
        'w_out': nrm(ks[3], (L, D_MODEL, D_MODEL), D_MODEL ** -0.5),
        'gm_ln_g': 1.0 + nrm(ks[4], (L, GM_WIDTH), 0.01),
        'gm_ln_b': nrm(ks[5], (L, GM_WIDTH), 0.01),
        'gm_spatial_w': nrm(ks[6], (L, GM_GROUPS, GM_CHUNK, GM_CHUNK), 0.5 * GM_CHUNK ** -0.5),
        'gm_spatial_b': 1.0 + nrm(ks[7], (L, GM_GROUPS, GM_CHUNK), 0.01),
        'cmp_pos_k': nrm(ks[8], (L, CMP_BLOCK, HEAD_DIM), 0.02),
        'cmp_w1_k': nrm(ks[9], (L, CMP_BLOCK * HEAD_DIM, CMP_HIDDEN), (CMP_BLOCK * HEAD_DIM) ** -0.5),
        'cmp_w2_k': nrm(ks[10], (L, CMP_HIDDEN, HEAD_DIM), CMP_HIDDEN ** -0.5),
        'cmp_pos_v': nrm(ks[11], (L, CMP_BLOCK, HEAD_DIM), 0.02),
        'cmp_w1_v': nrm(ks[12], (L, CMP_BLOCK * HEAD_DIM, CMP_HIDDEN), (CMP_BLOCK * HEAD_DIM) ** -0.5),
        'cmp_w2_v': nrm(ks[13], (L, CMP_HIDDEN, HEAD_DIM), CMP_HIDDEN ** -0.5),
        'norm_ffn_g': 1.0 + nrm(ks[14], (L, D_MODEL), 0.01),
        'peer_w_q': nrm(ks[15], (L, D_MODEL, PEER_HEADS * PEER_QUERY_DIM), D_MODEL ** -0.5),
        'peer_keys1': nrm(ks[16], (L, PEER_N_KEYS, PEER_QUERY_DIM // 2), (PEER_QUERY_DIM // 2) ** -0.5),
        'peer_keys2': nrm(ks[17], (L, PEER_N_KEYS, PEER_QUERY_DIM // 2), (PEER_QUERY_DIM // 2) ** -0.5),
        'peer_u': nrm(ks[18], (L, PEER_N_EXPERTS, D_MODEL), D_MODEL ** -0.5),
        'peer_v': nrm(ks[19], (L, PEER_N_EXPERTS, D_MODEL), PEER_HEADS ** -0.5),
        'norm_final_g': 1.0 + nrm(ks[20], (D_MODEL,), 0.01),
    }


def reference(x, norm_mix_g, w_in, w_out, gm_ln_g, gm_ln_b, gm_spatial_w, gm_spatial_b,
              cmp_pos_k, cmp_w1_k, cmp_w2_k, cmp_pos_v, cmp_w1_v, cmp_w2_v,
              norm_ffn_g, peer_w_q, peer_keys1, peer_keys2, peer_u, peer_v, norm_final_g):
    splits = [int(c) for c in np.cumsum(IN_SIZES)[:-1]]
    for l in range(DEPTH):
        h = rms_norm(x, norm_mix_g[l])
        proj = h @ w_in[l]
        u, v, q, kv, nsa_gate, merge_gate = jnp.split(proj, splits, axis=-1)
        o_a = gmlp_branch(u, v, gm_ln_g[l], gm_ln_b[l], gm_spatial_w[l], gm_spatial_b[l])
        o_b = nsa_branch(q, kv, nsa_gate, cmp_pos_k[l], cmp_w1_k[l], cmp_w2_k[l],
                         cmp_pos_v[l], cmp_w1_v[l], cmp_w2_v[l])
        g_a, g_b = jnp.split(jax.nn.sigmoid(merge_gate), 2, axis=-1)
        x = x + (g_a * o_a + g_b * o_b) @ w_out[l]
        x = x + peer(rms_norm(x, norm_ffn_g[l]), peer_w_q[l], peer_keys1[l], peer_keys2[l], peer_u[l], peer_v[l])
    return rms_norm(x, norm_final_g)
```

```python
import functools
import math

import numpy as np
import jax
import jax.numpy as jnp
from jax import lax
from jax.experimental import pallas as pl
from jax.experimental.pallas import tpu as pltpu

D_MODEL = 2048
GM_GROUPS = 16
GM_GROUP_DIM = 128
GM_CHUNK = 128

NSA_HEADS = 16
NSA_KV_HEADS = 4
NSA_GROUP = 4
HEAD_DIM = 128
CMP_BLOCK = 32
CMP_STRIDE = 16
CMP_HIDDEN = 256
SLC_BLOCK = 64
N_SELECT = 16
WINDOW = 512
ROPE_THETA = 10000.0

PEER_HEADS = 8
PEER_N_KEYS = 128
PEER_TOPK = 16
PEER_SLOTS = PEER_HEADS * PEER_TOPK

EPS = 1e-6
NEG_BIG = -1e30
POS_BIG = 1e30

VMEM_LIMIT_BYTES = 56 * 1024 * 1024

COL_U = 0
COL_V = 2048
COL_Q = 4096
COL_MERGE_A = 6144
COL_MERGE_B = 8192
COL_KV = 10240
COL_GATE = 13312
GATE_PAD = 128
COL_TOTAL = COL_GATE + NSA_KV_HEADS * GATE_PAD

BF16 = jnp.bfloat16
F32 = jnp.float32


def _params(*sem):
    return pltpu.CompilerParams(dimension_semantics=sem, vmem_limit_bytes=VMEM_LIMIT_BYTES)


def _gelu(x):
    return 0.5 * x * (1.0 + jnp.tanh(math.sqrt(2.0 / math.pi) * (x + 0.044715 * (x * x * x))))


def _sigmoid(x):
    return 1.0 / (1.0 + jnp.exp(-x))


def _rms(x, g):
    return x * lax.rsqrt(jnp.mean(x * x, axis=-1, keepdims=True) + EPS) * g


def _dot_nt(a, b, **kw):
    return lax.dot_general(a, b, (((1,), (1,)), ((), ())), preferred_element_type=F32, **kw)


def _rmsnorm_kernel(x_ref, g_ref, o_ref):
    o_ref[...] = _rms(x_ref[...], g_ref[...]).astype(o_ref.dtype)


def rmsnorm_cast(x, g, tm=512):
    T, D = x.shape
    return pl.pallas_call(
        _rmsnorm_kernel,
        grid=(T // tm,),
        in_specs=[pl.BlockSpec((tm, D), lambda i: (i, 0)), pl.BlockSpec((1, D), lambda i: (0, 0))],
        out_specs=pl.BlockSpec((tm, D), lambda i: (i, 0)),
        out_shape=jax.ShapeDtypeStruct((T, D), BF16),
        compiler_params=_params("parallel"),
        name="rmsnorm_cast",
    )(x, g.reshape(1, D))


def _mm_kernel(a_ref, w_ref, o_ref):
    o_ref[...] = jnp.dot(a_ref[...], w_ref[...], preferred_element_type=F32).astype(o_ref.dtype)


def matmul(a, w, tm, tn, out_dtype, name):
    M, K = a.shape
    N = w.shape[1]
    return pl.pallas_call(
        _mm_kernel,
        grid=(N // tn, M // tm),
        in_specs=[pl.BlockSpec((tm, K), lambda j, i: (i, 0)), pl.BlockSpec((K, tn), lambda j, i: (0, j))],
        out_specs=pl.BlockSpec((tm, tn), lambda j, i: (i, j)),
        out_shape=jax.ShapeDtypeStruct((M, N), out_dtype),
        compiler_params=_params("parallel", "parallel"),
        name=name,
    )(a, w)


def _gmlp_kernel(u_ref, v_ref, g_ref, b_ref, w_ref, bs_ref, o_ref):
    gv = _gelu(v_ref[...])
    mu = jnp.mean(gv, axis=-1, keepdims=True)
    xc = gv - mu
    vn = xc * lax.rsqrt(jnp.mean(xc * xc, axis=-1, keepdims=True) + EPS) * g_ref[...] + b_ref[...]
    vn = vn.astype(BF16)
    row = lax.broadcasted_iota(jnp.int32, (GM_CHUNK, GM_CHUNK), 0)
    col = lax.broadcasted_iota(jnp.int32, (GM_CHUNK, GM_CHUNK), 1)
    causal = row >= col
    bs = bs_ref[...]
    for g in range(GM_GROUPS):
        sl = slice(g * GM_GROUP_DIM, (g + 1) * GM_GROUP_DIM)
        w = jnp.where(causal, w_ref[g], 0.0).astype(BF16)
        z = jnp.dot(w, vn[:, sl], preferred_element_type=F32) + bs[:, g:g + 1]
        o_ref[:, sl] = (_gelu(u_ref[:, sl]) * z).astype(o_ref.dtype)


def gmlp_mixer(proj, ln_g, ln_b, w_s, b_s):
    T = proj.shape[0]
    W = GM_GROUPS * GM_GROUP_DIM
    return pl.pallas_call(
        _gmlp_kernel,
        grid=(T // GM_CHUNK,),
        in_specs=[
            pl.BlockSpec((GM_CHUNK, W), lambda i: (i, COL_U // W)),
            pl.BlockSpec((GM_CHUNK, W), lambda i: (i, COL_V // W)),
            pl.BlockSpec((1, W), lambda i: (0, 0)),
            pl.BlockSpec((1, W), lambda i: (0, 0)),
            pl.BlockSpec((GM_GROUPS, GM_CHUNK, GM_CHUNK), lambda i: (0, 0, 0)),
            pl.BlockSpec((GM_CHUNK, GM_GROUPS), lambda i: (0, 0)),
        ],
        out_specs=pl.BlockSpec((GM_CHUNK, W), lambda i: (i, 0)),
        out_shape=jax.ShapeDtypeStruct((T, W), BF16),
        compiler_params=_params("parallel"),
        name="gmlp_mixer",
    )(proj, proj, ln_g.reshape(1, W), ln_b.reshape(1, W), w_s, b_s.T)


def _rope(x, cos, sin_signed):
    return x * cos + pltpu.roll(x, HEAD_DIM // 2, axis=1) * sin_signed


def _compress(k_ref, pos_ref, w1_ref, w2_ref, nc):
    half = CMP_BLOCK // 2
    a = jnp.zeros((nc, CMP_HIDDEN), F32)
    b = jnp.zeros((nc, CMP_HIDDEN), F32)
    for l in range(half):
        kl = k_ref[0, pl.ds(l, nc, stride=CMP_STRIDE), :]
        a = a + jnp.dot((kl + pos_ref[l:l + 1, :]).astype(BF16), w1_ref[l], preferred_element_type=F32)
        b = b + jnp.dot((kl + pos_ref[half + l:half + l + 1, :]).astype(BF16), w1_ref[half + l],
                        preferred_element_type=F32)
    hid = a + pltpu.roll(b, nc - 1, axis=0)
    return jnp.dot(_gelu(hid).astype(BF16), w2_ref[...], preferred_element_type=F32)


def _nsa_prep_kernel(kc_ref, vc_ref, ks_ref, vs_ref, kw_ref, vw_ref, cos_ref, sin_ref,
                     pk_ref, w1k_ref, w2k_ref, pv_ref, w1v_ref, w2v_ref,
                     okc_ref, ovc_ref, oks_ref, ovs_ref, okw_ref, ovw_ref, *, nc):
    okc_ref[0, 0] = _compress(kc_ref, pk_ref, w1k_ref, w2k_ref, nc).astype(okc_ref.dtype)
    ovc_ref[0, 0] = _compress(vc_ref, pv_ref, w1v_ref, w2v_ref, nc).astype(ovc_ref.dtype)
    cos = cos_ref[...]
    sin = sin_ref[...]
    oks_ref[0, 0] = _rope(ks_ref[0], cos, sin).astype(oks_ref.dtype)
    okw_ref[0, 0] = _rope(kw_ref[0], cos, sin).astype(okw_ref.dtype)
    ovs_ref[0, 0] = vs_ref[0].astype(ovs_ref.dtype)
    ovw_ref[0, 0] = vw_ref[0].astype(ovw_ref.dtype)


def nsa_prep(proj3, cos, sin, pos_k, w1_k, w2_k, pos_v, w1_v, w2_v):
    B, S, _ = proj3.shape
    nc = S // CMP_STRIDE
    kvb = COL_KV // HEAD_DIM

    def kv_spec(i):
        return pl.BlockSpec((1, S, HEAD_DIM), lambda b, h, i=i: (b, 0, kvb + i * NSA_KV_HEADS + h))

    full2 = lambda shp: pl.BlockSpec(shp, lambda b, h: (0, 0))
    full3 = lambda shp: pl.BlockSpec(shp, lambda b, h: (0, 0, 0))
    out_c = pl.BlockSpec((1, 1, nc, HEAD_DIM), lambda b, h: (b, h, 0, 0))
    out_s = pl.BlockSpec((1, 1, S, HEAD_DIM), lambda b, h: (b, h, 0, 0))
    shp_c = jax.ShapeDtypeStruct((B, NSA_KV_HEADS, nc, HEAD_DIM), BF16)
    shp_s = jax.ShapeDtypeStruct((B, NSA_KV_HEADS, S, HEAD_DIM), BF16)
    w1k = w1_k.astype(BF16).reshape(CMP_BLOCK, HEAD_DIM, CMP_HIDDEN)
    w1v = w1_v.astype(BF16).reshape(CMP_BLOCK, HEAD_DIM, CMP_HIDDEN)
    return pl.pallas_call(
        functools.partial(_nsa_prep_kernel, nc=nc),
        grid=(B, NSA_KV_HEADS),
        in_specs=[kv_spec(i) for i in range(6)] + [
            full2((S, HEAD_DIM)), full2((S, HEAD_DIM)),
            full2((CMP_BLOCK, HEAD_DIM)), full3((CMP_BLOCK, HEAD_DIM, CMP_HIDDEN)), full2((CMP_HIDDEN, HEAD_DIM)),
            full2((CMP_BLOCK, HEAD_DIM)), full3((CMP_BLOCK, HEAD_DIM, CMP_HIDDEN)), full2((CMP_HIDDEN, HEAD_DIM)),
        ],
        out_specs=[out_c, out_c, out_s, out_s, out_s, out_s],
        out_shape=[shp_c, shp_c, shp_s, shp_s, shp_s, shp_s],
        compiler_params=_params("parallel", "parallel"),
        name="nsa_prep",
    )(proj3, proj3, proj3, proj3, proj3, proj3, cos, sin,
      pos_k, w1k, w2_k.astype(BF16), pos_v, w1v, w2_v.astype(BF16))


def _softmax_rows(s, mask):
    s = jnp.where(mask, s, NEG_BIG)
    m = jnp.max(s, axis=-1, keepdims=True)
    e = jnp.exp(s - m)
    return e / jnp.sum(e, axis=-1, keepdims=True)


def _nsa_attn_kernel(q_ref, gate_ref, kc_ref, vc_ref, ks_ref, vs_ref, kw_ref, vw_ref,
                     cos_ref, sin_ref, agg_ref, exp_ref, o_ref, *, tq, S, n_sel, wk):
    qt = pl.program_id(2)
    nc = S // CMP_STRIDE
    n_slc = S // SLC_BLOCK
    scale = HEAD_DIM ** -0.5
    pos = qt * tq + lax.broadcasted_iota(jnp.int32, (tq, 1), 0)
    cos = cos_ref[...]
    sin = sin_ref[...]
    gates = _sigmoid(gate_ref[0])

    qs = [q_ref[0, :, g * HEAD_DIM:(g + 1) * HEAD_DIM] for g in range(NSA_GROUP)]
    qr = [_rope(q, cos, sin).astype(BF16) for q in qs]

    kc = kc_ref[0, 0]
    vc = vc_ref[0, 0]
    cmp_end = lax.broadcasted_iota(jnp.int32, (1, nc), 1) * CMP_STRIDE + (CMP_BLOCK - 1)
    cmask = (cmp_end <= pos) & (cmp_end < S)
    any_c = pos >= CMP_BLOCK - 1
    o_cmp = []
    imp = jnp.zeros((tq, nc), F32)
    for g in range(NSA_GROUP):
        s = _dot_nt(qs[g].astype(BF16), kc) * scale
        p = jnp.where(any_c, _softmax_rows(s, cmask), 0.0)
        o_cmp.append(jnp.dot(p.astype(BF16), vc, preferred_element_type=F32))
        imp = imp + p

    imp_slc = jnp.dot(imp, agg_ref[...], preferred_element_type=F32, precision=lax.Precision.HIGHEST)
    blk = lax.broadcasted_iota(jnp.int32, (1, n_slc), 1)
    allowed = blk * SLC_BLOCK <= pos
    forced = (blk == 0) | (blk == pos // SLC_BLOCK)
    score = jnp.where(forced, POS_BIG, jnp.where(allowed, imp_slc, NEG_BIG))
    rank = jnp.zeros((tq, n_slc), F32)
    for i in range(n_slc):
        si = score[:, i:i + 1]
        earlier = (blk > i).astype(F32)
        rank = rank + jnp.where(si > score, 1.0, jnp.where(si == score, earlier, 0.0))
    sel = jnp.where(rank < n_sel, 1.0, 0.0).astype(BF16)
    kpos = lax.broadcasted_iota(jnp.int32, (1, S), 1)
    smask = (jnp.dot(sel, exp_ref[...], preferred_element_type=F32) > 0.5) & (kpos <= pos)

    ks = ks_ref[0, 0]
    vs = vs_ref[0, 0]
    o_slc = []
    for g in range(NSA_GROUP):
        s = _dot_nt(qr[g], ks) * scale
        p = _softmax_rows(s, smask)
        o_slc.append(jnp.dot(p.astype(BF16), vs, preferred_element_type=F32))

    start = pl.multiple_of(jnp.maximum(qt * tq - (wk - tq), 0), tq)
    kw = kw_ref[0, 0, pl.ds(start, wk), :]
    vw = vw_ref[0, 0, pl.ds(start, wk), :]
    diff = pos - (start + lax.broadcasted_iota(jnp.int32, (1, wk), 1))
    wmask = (diff >= 0) & (diff < WINDOW)
    for g in range(NSA_GROUP):
        s = _dot_nt(qr[g], kw) * scale
        p = _softmax_rows(s, wmask)
        o_win = jnp.dot(p.astype(BF16), vw, preferred_element_type=F32)
        c = g * 3
        o = gates[:, c:c + 1] * o_cmp[g] + gates[:, c + 1:c + 2] * o_slc[g] + gates[:, c + 2:c + 3] * o_win
        o_ref[0, :, g * HEAD_DIM:(g + 1) * HEAD_DIM] = o.astype(o_ref.dtype)


def _selection_constants(S):
    nc = S // CMP_STRIDE
    n_cmp = (S - CMP_BLOCK) // CMP_STRIDE + 1
    n_slc = S // SLC_BLOCK
    ratio = SLC_BLOCK // CMP_STRIDE
    span = CMP_BLOCK // CMP_STRIDE
    agg_w = np.convolve(np.ones(ratio), np.ones(span))
    agg = np.zeros((nc, n_slc), np.float32)
    for j in range(n_slc):
        for w in range(ratio + span - 1):
            c = ratio * j + w - (span - 1)
            if 0 <= c < n_cmp:
                agg[c, j] += agg_w[w]
    expand = (np.arange(S)[None, :] // SLC_BLOCK == np.arange(n_slc)[:, None]).astype(np.float32)
    return jnp.asarray(agg), jnp.asarray(expand, dtype=BF16)


def nsa_attention(proj3, gate3, kc, vc, ks, vs, kw, vw, cos, sin, tq=256):
    B, S, _ = proj3.shape
    nc = S // CMP_STRIDE
    n_slc = S // SLC_BLOCK
    n_sel = min(N_SELECT, n_slc)
    wk = min(S, WINDOW + tq)
    agg, expand = _selection_constants(S)
    qw = NSA_GROUP * HEAD_DIM
    cspec = pl.BlockSpec((1, 1, nc, HEAD_DIM), lambda b, h, t: (b, h, 0, 0))
    sspec = pl.BlockSpec((1, 1, S, HEAD_DIM), lambda b, h, t: (b, h, 0, 0))
    return pl.pallas_call(
        functools.partial(_nsa_attn_kernel, tq=tq, S=S, n_sel=n_sel, wk=wk),
        grid=(B, NSA_KV_HEADS, S // tq),
        in_specs=[
            pl.BlockSpec((1, tq, qw), lambda b, h, t: (b, t, COL_Q // qw + h)),
            pl.BlockSpec((1, tq, GATE_PAD), lambda b, h, t: (b, t, COL_GATE // GATE_PAD + h)),
            cspec, cspec, sspec, sspec, sspec, sspec,
            pl.BlockSpec((tq, HEAD_DIM), lambda b, h, t: (t, 0)),
            pl.BlockSpec((tq, HEAD_DIM), lambda b, h, t: (t, 0)),
            pl.BlockSpec((nc, n_slc), lambda b, h, t: (0, 0)),
            pl.BlockSpec((n_slc, S), lambda b, h, t: (0, 0)),
        ],
        out_specs=pl.BlockSpec((1, tq, qw), lambda b, h, t: (b, t, h)),
        out_shape=jax.ShapeDtypeStruct((B, S, NSA_HEADS * HEAD_DIM), BF16),
        compiler_params=_params("parallel", "parallel", "parallel"),
        name="nsa_attention",
    )(proj3, gate3, kc, vc, ks, vs, kw, vw, cos, sin, agg, expand)


def _merge_out_kernel(ma_ref, mb_ref, oa_ref, ob_ref, x_ref, w_ref, g_ref, x1_ref, h_ref):
    m = _sigmoid(ma_ref[...]) * oa_ref[...].astype(F32) + _sigmoid(mb_ref[...]) * ob_ref[...].astype(F32)
    x1 = x_ref[...] + jnp.dot(m.astype(BF16), w_ref[...], preferred_element_type=F32)
    x1_ref[...] = x1
    h_ref[...] = _rms(x1, g_ref[...])


def merge_out(proj, o_a, o_b, x, w_out, g, tm=256):
    T, D = x.shape
    row = lambda c: pl.BlockSpec((tm, D), lambda i, c=c: (i, c))
    return pl.pallas_call(
        _merge_out_kernel,
        grid=(T // tm,),
        in_specs=[row(COL_MERGE_A // D), row(COL_MERGE_B // D), row(0), row(0), row(0),
                  pl.BlockSpec((D, D), lambda i: (0, 0)), pl.BlockSpec((1, D), lambda i: (0, 0))],
        out_specs=[row(0), row(0)],
        out_shape=[jax.ShapeDtypeStruct((T, D), F32), jax.ShapeDtypeStruct((T, D), F32)],
        compiler_params=_params("parallel"),
        name="merge_out",
    )(proj, proj, o_a, o_b, x, w_out, g.reshape(1, D))


def _topk_rows(vals, payload, k):
    n_rows = vals.shape[0]
    row = lax.broadcasted_iota(jnp.int32, vals.shape, 0)
    out_v, out_p = [], []
    for _ in range(k):
        m = jnp.max(vals, axis=0, keepdims=True)
        first = jnp.min(jnp.where(vals == m, row, n_rows), axis=0, keepdims=True)
        hit = row == first
        out_v.append(m)
        out_p.append(jnp.max(jnp.where(hit, payload, -1), axis=0, keepdims=True))
        vals = jnp.where(hit, -jnp.inf, vals)
    return jnp.concatenate(out_v, axis=0), jnp.concatenate(out_p, axis=0)


def _peer_route_kernel(q_ref, k1_ref, k2_ref, e_ref, g_ref):
    tt = q_ref.shape[0]
    key_row = lax.broadcasted_iota(jnp.int32, (PEER_N_KEYS, tt), 0)
    for h in range(PEER_HEADS):
        tops = []
        for c, k_ref in enumerate((k1_ref, k2_ref)):
            qh = q_ref[:, (2 * h + c) * PEER_N_KEYS:(2 * h + c + 1) * PEER_N_KEYS]
            s = _dot_nt(k_ref[...], qh, precision=lax.Precision.HIGHEST)
            tops.append(_topk_rows(s, key_row, PEER_TOPK))
        (v1, i1), (v2, i2) = tops
        cand = jnp.concatenate([v1[a:a + 1, :] + v2 for a in range(PEER_TOPK)], axis=0)
        cand_id = jnp.concatenate([i1[a:a + 1, :] * PEER_N_KEYS + i2 for a in range(PEER_TOPK)], axis=0)
        top_s, top_e = _topk_rows(cand, cand_id, PEER_TOPK)
        ex = jnp.exp(top_s - jnp.max(top_s, axis=0, keepdims=True))
        e_ref[h * PEER_TOPK:(h + 1) * PEER_TOPK, :] = top_e
        g_ref[h * PEER_TOPK:(h + 1) * PEER_TOPK, :] = ex / jnp.sum(ex, axis=0, keepdims=True)


def peer_route(qp, keys1, keys2, tt=256):
    T, Q = qp.shape
    kspec = pl.BlockSpec(keys1.shape, lambda i: (0, 0))
    ospec = pl.BlockSpec((PEER_SLOTS, tt), lambda i: (0, i))
    return pl.pallas_call(
        _peer_route_kernel,
        grid=(T // tt,),
        in_specs=[pl.BlockSpec((tt, Q), lambda i: (i, 0)), kspec, kspec],
        out_specs=[ospec, ospec],
        out_shape=[jax.ShapeDtypeStruct((PEER_SLOTS, T), jnp.int32), jax.ShapeDtypeStruct((PEER_SLOTS, T), F32)],
        compiler_params=_params("parallel"),
        name="peer_route",
    )(qp, keys1, keys2)


ROWS = 8


def _peer_apply_kernel(idx_ref, h_ref, gt_ref, uv_hbm, o_ref, buf, xg_ref, og_ref, sem, *, tb):
    D = D_MODEL

    def row_copy(t, j, slot):
        e = idx_ref[t, j]
        return pltpu.make_async_copy(uv_hbm.at[pl.ds(e, 1), :], buf.at[slot, pl.ds(j, 1), :], sem.at[slot])

    def issue(t, slot):
        def body(j, carry):
            row_copy(t, j, slot).start()
            return carry
        lax.fori_loop(0, PEER_SLOTS, body, 0)

    def wait_all(slot):
        pltpu.make_async_copy(uv_hbm.at[pl.ds(0, PEER_SLOTS), :], buf.at[slot], sem.at[slot]).wait()

    issue(0, 0)
    lane = lax.broadcasted_iota(jnp.int32, (PEER_SLOTS, tb), 1)

    def token(t, i, slot):
        if i + 1 < ROWS:
            issue(t + 1, 1 - slot)
        else:
            @pl.when(t + 1 < tb)
            def _():
                issue(t + 1, 1 - slot)

        wait_all(slot)
        acc = jnp.zeros((PEER_SLOTS, 128), F32)
        for c in range(D // 128):
            acc = acc + buf[slot, :, c * 128:(c + 1) * 128] * xg_ref[i:i + 1, c * 128:(c + 1) * 128]
        act = _gelu(jnp.sum(acc, axis=1, keepdims=True))
        gate = jnp.sum(jnp.where(lane == t, gt_ref[...], 0.0), axis=1, keepdims=True)
        w = gate * act
        for c in range(D // 128):
            og_ref[i:i + 1, c * 128:(c + 1) * 128] = jnp.sum(
                buf[slot, :, D + c * 128:D + (c + 1) * 128] * w, axis=0, keepdims=True)

    def group(g, carry):
        t0 = pl.multiple_of(g * ROWS, ROWS)
        xg_ref[...] = h_ref[pl.ds(t0, ROWS), :]
        for i in range(ROWS):
            token(t0 + i, i, i % 2)
        o_ref[pl.ds(t0, ROWS), :] = og_ref[...]
        return carry

    lax.fori_loop(0, tb // ROWS, group, 0)


def peer_apply(experts, h, gate_t, uv, tb=128):
    T, D = h.shape
    return pl.pallas_call(
        functools.partial(_peer_apply_kernel, tb=tb),
        grid=(T // tb,),
        in_specs=[
            pl.BlockSpec((tb, PEER_SLOTS), lambda i: (i, 0), memory_space=pltpu.SMEM),
            pl.BlockSpec((tb, D), lambda i: (i, 0)),
            pl.BlockSpec((PEER_SLOTS, tb), lambda i: (0, i)),
            pl.BlockSpec(memory_space=pl.ANY),
        ],
        out_specs=pl.BlockSpec((tb, D), lambda i: (i, 0)),
        out_shape=jax.ShapeDtypeStruct((T, D), F32),
        scratch_shapes=[pltpu.VMEM((2, PEER_SLOTS, 2 * D), F32), pltpu.VMEM((ROWS, D), F32),
                        pltpu.VMEM((ROWS, D), F32), pltpu.SemaphoreType.DMA((2,))],
        compiler_params=_params("arbitrary"),
        name="peer_apply",
    )(experts, h, gate_t, uv)


def _final_kernel(x_ref, y_ref, g_ref, o_ref):
    o_ref[...] = _rms(x_ref[...] + y_ref[...], g_ref[...])


def final_norm(x1, y, g, tm=512):
    T, D = x1.shape
    spec = pl.BlockSpec((tm, D), lambda i: (i, 0))
    return pl.pallas_call(
        _final_kernel,
        grid=(T // tm,),
        in_specs=[spec, spec, pl.BlockSpec((1, D), lambda i: (0, 0))],
        out_specs=spec,
        out_shape=jax.ShapeDtypeStruct((T, D), F32),
        compiler_params=_params("parallel"),
        name="final_norm",
    )(x1, y, g.reshape(1, D))


def _pack_w_in(w):
    u, v, q, kv, gate, merge = jnp.split(w, [2048, 4096, 6144, 9216, 9264], axis=1)
    gate = gate.reshape(D_MODEL, NSA_KV_HEADS, NSA_GROUP * 3)
    gate = jnp.pad(gate, ((0, 0), (0, 0), (0, GATE_PAD - NSA_GROUP * 3))).reshape(D_MODEL, NSA_KV_HEADS * GATE_PAD)
    return jnp.concatenate([u, v, q, merge, kv, gate], axis=1).astype(BF16)


def _rope_tables(S):
    half = HEAD_DIM // 2
    inv = ROPE_THETA ** (-jnp.arange(half, dtype=F32) / half)
    ang = jnp.arange(S, dtype=F32)[:, None] * inv[None, :]
    cos = jnp.cos(ang)
    sin = jnp.sin(ang)
    return jnp.concatenate([cos, cos], axis=1), jnp.concatenate([-sin, sin], axis=1)


def _layer(x, norm_mix_g, w_in, w_out, gm_ln_g, gm_ln_b, gm_spatial_w, gm_spatial_b,
           cmp_pos_k, cmp_w1_k, cmp_w2_k, cmp_pos_v, cmp_w1_v, cmp_w2_v,
           norm_ffn_g, peer_w_q, peer_keys1, peer_keys2, peer_u, peer_v):
    B, S, D = x.shape
    T = B * S
    xt = x.reshape(T, D)
    h = rmsnorm_cast(xt, norm_mix_g)
    proj = matmul(h, _pack_w_in(w_in), 1024, 1536, F32, "in_proj")
    o_a = gmlp_mixer(proj, gm_ln_g, gm_ln_b, gm_spatial_w, gm_spatial_b)
    proj3 = proj.reshape(B, S, COL_TOTAL)
    cos, sin = _rope_tables(S)
    kc, vc, ks, vs, kw, vw = nsa_prep(proj3, cos, sin, cmp_pos_k, cmp_w1_k, cmp_w2_k, cmp_pos_v, cmp_w1_v, cmp_w2_v)
    o_b = nsa_attention(proj3, proj3, kc, vc, ks, vs, kw, vw, cos, sin).reshape(T, D)
    x1, h2 = merge_out(proj, o_a, o_b, xt, w_out.astype(BF16), norm_ffn_g)
    qp = matmul(h2.astype(BF16), peer_w_q.astype(BF16), 1024, 1024, F32, "peer_query")
    experts_t, gate_t = peer_route(qp, peer_keys1, peer_keys2)
    uv = jnp.concatenate([peer_u, peer_v], axis=1)
    y = peer_apply(experts_t.T, h2, gate_t, uv)
    return x1, y


def kernel(x, norm_mix_g, w_in, w_out, gm_ln_g, gm_ln_b, gm_spatial_w, gm_spatial_b, cmp_pos_k, cmp_w1_k, cmp_w2_k, cmp_pos_v, cmp_w1_v, cmp_w2_v, norm_ffn_g, peer_w_q, peer_keys1, peer_keys2, peer_u, peer_v, norm_final_g):
    B, S, D = x.shape
    depth = w_in.shape[0]
    y = None
    for l in range(depth):
        if y is not None:
            x = (x.reshape(B * S, D) + y).reshape(B, S, D)
        x1, y = _layer(x, norm_mix_g[l], w_in[l], w_out[l], gm_ln_g[l], gm_ln_b[l], gm_spatial_w[l], gm_spatial_b[l],
                       cmp_pos_k[l], cmp_w1_k[l], cmp_w2_k[l], cmp_pos_v[l], cmp_w1_v[l], cmp_w2_v[l],
                       norm_ffn_g[l], peer_w_q[l], peer_keys1[l], peer_keys2[l], peer_u[l], peer_v[l])
        x = x1.reshape(B, S, D)
    return final_norm(x.reshape(B * S, D), y, norm_final_g).reshape(B, S, D)
```

```python
import functools
import math

import numpy as np
import jax
import jax.numpy as jnp
from jax import lax
from jax.experimental import pallas as pl
from jax.experimental.pallas import tpu as pltpu

D_MODEL = 2048
GM_GROUPS = 16
GM_GROUP_DIM = 128
GM_CHUNK = 128

NSA_HEADS = 16
NSA_KV_HEADS = 4
NSA_GROUP = 4
HEAD_DIM = 128
CMP_BLOCK = 32
CMP_STRIDE = 16
CMP_HIDDEN = 256
SLC_BLOCK = 64
N_SELECT = 16
WINDOW = 512
ROPE_THETA = 10000.0

PEER_HEADS = 8
PEER_N_KEYS = 128
PEER_TOPK = 16
PEER_SLOTS = PEER_HEADS * PEER_TOPK

EPS = 1e-6
NEG_BIG = -1e30
POS_BIG = 1e30

VMEM_LIMIT_BYTES = 56 * 1024 * 1024

COL_U = 0
COL_V = 2048
COL_Q = 4096
COL_MERGE_A = 6144
COL_MERGE_B = 8192
COL_KV = 10240
COL_GATE = 13312
GATE_PAD = 128
COL_TOTAL = COL_GATE + NSA_KV_HEADS * GATE_PAD

BF16 = jnp.bfloat16
F32 = jnp.float32


def _params(*sem):
    return pltpu.CompilerParams(dimension_semantics=sem, vmem_limit_bytes=VMEM_LIMIT_BYTES)


def _gelu(x):
    return 0.5 * x * (1.0 + jnp.tanh(math.sqrt(2.0 / math.pi) * (x + 0.044715 * (x * x * x))))


def _sigmoid(x):
    return 1.0 / (1.0 + jnp.exp(-x))


def _rms(x, g):
    return x * lax.rsqrt(jnp.mean(x * x, axis=-1, keepdims=True) + EPS) * g


def _dot_nt(a, b, **kw):
    return lax.dot_general(a, b, (((1,), (1,)), ((), ())), preferred_element_type=F32, **kw)


def _rmsnorm_kernel(x_ref, g_ref, o_ref):
    o_ref[...] = _rms(x_ref[...], g_ref[...]).astype(o_ref.dtype)


def rmsnorm_cast(x, g, tm=512):
    T, D = x.shape
    return pl.pallas_call(
        _rmsnorm_kernel,
        grid=(T // tm,),
        in_specs=[pl.BlockSpec((tm, D), lambda i: (i, 0)), pl.BlockSpec((1, D), lambda i: (0, 0))],
        out_specs=pl.BlockSpec((tm, D), lambda i: (i, 0)),
        out_shape=jax.ShapeDtypeStruct((T, D), BF16),
        compiler_params=_params("parallel"),
        name="rmsnorm_cast",
    )(x, g.reshape(1, D))


def _mm_kernel(a_ref, w_ref, o_ref):
    o_ref[...] = jnp.dot(a_ref[...], w_ref[...], preferred_element_type=F32).astype(o_ref.dtype)


def matmul(a, w, tm, tn, out_dtype, name):
    M, K = a.shape
    N = w.shape[1]
    return pl.pallas_call(
        _mm_kernel,
        grid=(N // tn, M // tm),
        in_specs=[pl.BlockSpec((tm, K), lambda j, i: (i, 0)), pl.BlockSpec((K, tn), lambda j, i: (0, j))],
        out_specs=pl.BlockSpec((tm, tn), lambda j, i: (i, j)),
        out_shape=jax.ShapeDtypeStruct((M, N), out_dtype),
        compiler_params=_params("parallel", "parallel"),
        name=name,
    )(a, w)


def _gmlp_kernel(u_ref, v_ref, g_ref, b_ref, w_ref, bs_ref, o_ref):
    gv = _gelu(v_ref[...])
    mu = jnp.mean(gv, axis=-1, keepdims=True)
    xc = gv - mu
    vn = xc * lax.rsqrt(jnp.mean(xc * xc, axis=-1, keepdims=True) + EPS) * g_ref[...] + b_ref[...]
    vn = vn.astype(BF16)
    row = lax.broadcasted_iota(jnp.int32, (GM_CHUNK, GM_CHUNK), 0)
    col = lax.broadcasted_iota(jnp.int32, (GM_CHUNK, GM_CHUNK), 1)
    causal = row >= col
    bs = bs_ref[...]
    for g in range(GM_GROUPS):
        sl = slice(g * GM_GROUP_DIM, (g + 1) * GM_GROUP_DIM)
        w = jnp.where(causal, w_ref[g], 0.0).astype(BF16)
        z = jnp.dot(w, vn[:, sl], preferred_element_type=F32) + bs[:, g:g + 1]
        o_ref[:, sl] = (_gelu(u_ref[:, sl]) * z).astype(o_ref.dtype)


def gmlp_mixer(proj, ln_g, ln_b, w_s, b_s):
    T = proj.shape[0]
    W = GM_GROUPS * GM_GROUP_DIM
    return pl.pallas_call(
        _gmlp_kernel,
        grid=(T // GM_CHUNK,),
        in_specs=[
            pl.BlockSpec((GM_CHUNK, W), lambda i: (i, COL_U // W)),
            pl.BlockSpec((GM_CHUNK, W), lambda i: (i, COL_V // W)),
            pl.BlockSpec((1, W), lambda i: (0, 0)),
            pl.BlockSpec((1, W), lambda i: (0, 0)),
            pl.BlockSpec((GM_GROUPS, GM_CHUNK, GM_CHUNK), lambda i: (0, 0, 0)),
            pl.BlockSpec((GM_CHUNK, GM_GROUPS), lambda i: (0, 0)),
        ],
        out_specs=pl.BlockSpec((GM_CHUNK, W), lambda i: (i, 0)),
        out_shape=jax.ShapeDtypeStruct((T, W), BF16),
        compiler_params=_params("parallel"),
        name="gmlp_mixer",
    )(proj, proj, ln_g.reshape(1, W), ln_b.reshape(1, W), w_s, b_s.T)


def _rope(x, cos, sin_signed):
    return x * cos + pltpu.roll(x, HEAD_DIM // 2, axis=1) * sin_signed


def _compress(k_ref, pos_ref, w1_ref, w2_ref, nc):
    half = CMP_BLOCK // 2
    a = jnp.zeros((nc, CMP_HIDDEN), F32)
    b = jnp.zeros((nc, CMP_HIDDEN), F32)
    for l in range(half):
        kl = k_ref[0, pl.ds(l, nc, stride=CMP_STRIDE), :]
        a = a + jnp.dot((kl + pos_ref[l:l + 1, :]).astype(BF16), w1_ref[l], preferred_element_type=F32)
        b = b + jnp.dot((kl + pos_ref[half + l:half + l + 1, :]).astype(BF16), w1_ref[half + l],
                        preferred_element_type=F32)
    hid = a + pltpu.roll(b, nc - 1, axis=0)
    return jnp.dot(_gelu(hid).astype(BF16), w2_ref[...], preferred_element_type=F32)


def _nsa_prep_kernel(kc_ref, vc_ref, ks_ref, vs_ref, kw_ref, vw_ref, cos_ref, sin_ref,
                     pk_ref, w1k_ref, w2k_ref, pv_ref, w1v_ref, w2v_ref,
                     okc_ref, ovc_ref, oks_ref, ovs_ref, okw_ref, ovw_ref, *, nc):
    okc_ref[0, 0] = _compress(kc_ref, pk_ref, w1k_ref, w2k_ref, nc).astype(okc_ref.dtype)
    ovc_ref[0, 0] = _compress(vc_ref, pv_ref, w1v_ref, w2v_ref, nc).astype(ovc_ref.dtype)
    cos = cos_ref[...]
    sin = sin_ref[...]
    oks_ref[0, 0] = _rope(ks_ref[0], cos, sin).astype(oks_ref.dtype)
    okw_ref[0, 0] = _rope(kw_ref[0], cos, sin).astype(okw_ref.dtype)
    ovs_ref[0, 0] = vs_ref[0].astype(ovs_ref.dtype)
    ovw_ref[0, 0] = vw_ref[0].astype(ovw_ref.dtype)


def nsa_prep(proj3, cos, sin, pos_k, w1_k, w2_k, pos_v, w1_v, w2_v):
    B, S, _ = proj3.shape
    nc = S // CMP_STRIDE
    kvb = COL_KV // HEAD_DIM

    def kv_spec(i):
        return pl.BlockSpec((1, S, HEAD_DIM), lambda b, h, i=i: (b, 0, kvb + i * NSA_KV_HEADS + h))

    full2 = lambda shp: pl.BlockSpec(shp, lambda b, h: (0, 0))
    full3 = lambda shp: pl.BlockSpec(shp, lambda b, h: (0, 0, 0))
    out_c = pl.BlockSpec((1, 1, nc, HEAD_DIM), lambda b, h: (b, h, 0, 0))
    out_s = pl.BlockSpec((1, 1, S, HEAD_DIM), lambda b, h: (b, h, 0, 0))
    shp_c = jax.ShapeDtypeStruct((B, NSA_KV_HEADS, nc, HEAD_DIM), BF16)
    shp_s = jax.ShapeDtypeStruct((B, NSA_KV_HEADS, S, HEAD_DIM), BF16)
    w1k = w1_k.astype(BF16).reshape(CMP_BLOCK, HEAD_DIM, CMP_HIDDEN)
    w1v = w1_v.astype(BF16).reshape(CMP_BLOCK, HEAD_DIM, CMP_HIDDEN)
    return pl.pallas_call(
        functools.partial(_nsa_prep_kernel, nc=nc),
        grid=(B, NSA_KV_HEADS),
        in_specs=[kv_spec(i) for i in range(6)] + [
            full2((S, HEAD_DIM)), full2((S, HEAD_DIM)),
            full2((CMP_BLOCK, HEAD_DIM)), full3((CMP_BLOCK, HEAD_DIM, CMP_HIDDEN)), full2((CMP_HIDDEN, HEAD_DIM)),
            full2((CMP_BLOCK, HEAD_DIM)), full3((CMP_BLOCK, HEAD_DIM, CMP_HIDDEN)), full2((CMP_HIDDEN, HEAD_DIM)),
        ],
        out_specs=[out_c, out_c, out_s, out_s, out_s, out_s],
        out_shape=[shp_c, shp_c, shp_s, shp_s, shp_s, shp_s],
        compiler_params=_params("parallel", "parallel"),
        name="nsa_prep",
    )(proj3, proj3, proj3, proj3, proj3, proj3, cos, sin,
      pos_k, w1k, w2_k.astype(BF16), pos_v, w1v, w2_v.astype(BF16))


def _softmax_rows(s, mask):
    s = jnp.where(mask, s, NEG_BIG)
    m = jnp.max(s, axis=-1, keepdims=True)
    e = jnp.exp(s - m)
    return e / jnp.sum(e, axis=-1, keepdims=True)


def _nsa_attn_kernel(q_ref, gate_ref, kc_ref, vc_ref, ks_ref, vs_ref, kw_ref, vw_ref,
                     cos_ref, sin_ref, agg_ref, exp_ref, o_ref, *, tq, S, n_sel, wk):
    qt = pl.program_id(2)
    nc = S // CMP_STRIDE
    n_slc = S // SLC_BLOCK
    scale = HEAD_DIM ** -0.5
    pos = qt * tq + lax.broadcasted_iota(jnp.int32, (tq, 1), 0)
    cos = cos_ref[...]
    sin = sin_ref[...]
    gates = _sigmoid(gate_ref[0])

    qs = [q_ref[0, :, g * HEAD_DIM:(g + 1) * HEAD_DIM] for g in range(NSA_GROUP)]
    qr = [_rope(q, cos, sin).astype(BF16) for q in qs]

    kc = kc_ref[0, 0]
    vc = vc_ref[0, 0]
    cmp_end = lax.broadcasted_iota(jnp.int32, (1, nc), 1) * CMP_STRIDE + (CMP_BLOCK - 1)
    cmask = (cmp_end <= pos) & (cmp_end < S)
    any_c = pos >= CMP_BLOCK - 1
    o_cmp = []
    imp = jnp.zeros((tq, nc), F32)
    for g in range(NSA_GROUP):
        s = _dot_nt(qs[g].astype(BF16), kc) * scale
        p = jnp.where(any_c, _softmax_rows(s, cmask), 0.0)
        o_cmp.append(jnp.dot(p.astype(BF16), vc, preferred_element_type=F32))
        imp = imp + p

    imp_slc = jnp.dot(imp, agg_ref[...], preferred_element_type=F32, precision=lax.Precision.HIGHEST)
    blk = lax.broadcasted_iota(jnp.int32, (1, n_slc), 1)
    allowed = blk * SLC_BLOCK <= pos
    forced = (blk == 0) | (blk == pos // SLC_BLOCK)
    score = jnp.where(forced, POS_BIG, jnp.where(allowed, imp_slc, NEG_BIG))
    rank = jnp.zeros((tq, n_slc), F32)
    for i in range(n_slc):
        si = score[:, i:i + 1]
        earlier = (blk > i).astype(F32)
        rank = rank + jnp.where(si > score, 1.0, jnp.where(si == score, earlier, 0.0))
    sel = jnp.where(rank < n_sel, 1.0, 0.0).astype(BF16)
    kpos = lax.broadcasted_iota(jnp.int32, (1, S), 1)
    smask = (jnp.dot(sel, exp_ref[...], preferred_element_type=F32) > 0.5) & (kpos <= pos)

    ks = ks_ref[0, 0]
    vs = vs_ref[0, 0]
    o_slc = []
    for g in range(NSA_GROUP):
        s = _dot_nt(qr[g], ks) * scale
        p = _softmax_rows(s, smask)
        o_slc.append(jnp.dot(p.astype(BF16), vs, preferred_element_type=F32))

    start = pl.multiple_of(jnp.maximum(qt * tq - (wk - tq), 0), tq)
    kw = kw_ref[0, 0, pl.ds(start, wk), :]
    vw = vw_ref[0, 0, pl.ds(start, wk), :]
    diff = pos - (start + lax.broadcasted_iota(jnp.int32, (1, wk), 1))
    wmask = (diff >= 0) & (diff < WINDOW)
    for g in range(NSA_GROUP):
        s = _dot_nt(qr[g], kw) * scale
        p = _softmax_rows(s, wmask)
        o_win = jnp.dot(p.astype(BF16), vw, preferred_element_type=F32)
        c = g * 3
        o = gates[:, c:c + 1] * o_cmp[g] + gates[:, c + 1:c + 2] * o_slc[g] + gates[:, c + 2:c + 3] * o_win
        o_ref[0, :, g * HEAD_DIM:(g + 1) * HEAD_DIM] = o.astype(o_ref.dtype)


def _selection_constants(S):
    nc = S // CMP_STRIDE
    n_cmp = (S - CMP_BLOCK) // CMP_STRIDE + 1
    n_slc = S // SLC_BLOCK
    ratio = SLC_BLOCK // CMP_STRIDE
    span = CMP_BLOCK // CMP_STRIDE
    agg_w = np.convolve(np.ones(ratio), np.ones(span))
    agg = np.zeros((nc, n_slc), np.float32)
    for j in range(n_slc):
        for w in range(ratio + span - 1):
            c = ratio * j + w - (span - 1)
            if 0 <= c < n_cmp:
                agg[c, j] += agg_w[w]
    expand = (np.arange(S)[None, :] // SLC_BLOCK == np.arange(n_slc)[:, None]).astype(np.float32)
    return jnp.asarray(agg), jnp.asarray(expand, dtype=BF16)


def nsa_attention(proj3, gate3, kc, vc, ks, vs, kw, vw, cos, sin, tq=256):
    B, S, _ = proj3.shape
    nc = S // CMP_STRIDE
    n_slc = S // SLC_BLOCK
    n_sel = min(N_SELECT, n_slc)
    wk = min(S, WINDOW + tq)
    agg, expand = _selection_constants(S)
    qw = NSA_GROUP * HEAD_DIM
    cspec = pl.BlockSpec((1, 1, nc, HEAD_DIM), lambda b, h, t: (b, h, 0, 0))
    sspec = pl.BlockSpec((1, 1, S, HEAD_DIM), lambda b, h, t: (b, h, 0, 0))
    return pl.pallas_call(
        functools.partial(_nsa_attn_kernel, tq=tq, S=S, n_sel=n_sel, wk=wk),
        grid=(B, NSA_KV_HEADS, S // tq),
        in_specs=[
            pl.BlockSpec((1, tq, qw), lambda b, h, t: (b, t, COL_Q // qw + h)),
            pl.BlockSpec((1, tq, GATE_PAD), lambda b, h, t: (b, t, COL_GATE // GATE_PAD + h)),
            cspec, cspec, sspec, sspec, sspec, sspec,
            pl.BlockSpec((tq, HEAD_DIM), lambda b, h, t: (t, 0)),
            pl.BlockSpec((tq, HEAD_DIM), lambda b, h, t: (t, 0)),
            pl.BlockSpec((nc, n_slc), lambda b, h, t: (0, 0)),
            pl.BlockSpec((n_slc, S), lambda b, h, t: (0, 0)),
        ],
        out_specs=pl.BlockSpec((1, tq, qw), lambda b, h, t: (b, t, h)),
        out_shape=jax.ShapeDtypeStruct((B, S, NSA_HEADS * HEAD_DIM), BF16),
        compiler_params=_params("parallel", "parallel", "parallel"),
        name="nsa_attention",
    )(proj3, gate3, kc, vc, ks, vs, kw, vw, cos, sin, agg, expand)


def _merge_out_kernel(ma_ref, mb_ref, oa_ref, ob_ref, x_ref, w_ref, g_ref, x1_ref, h_ref):
    m = _sigmoid(ma_ref[...]) * oa_ref[...].astype(F32) + _sigmoid(mb_ref[...]) * ob_ref[...].astype(F32)
    x1 = x_ref[...] + jnp.dot(m.astype(BF16), w_ref[...], preferred_element_type=F32)
    x1_ref[...] = x1
    h_ref[...] = _rms(x1, g_ref[...])


def merge_out(proj, o_a, o_b, x, w_out, g, tm=256):
    T, D = x.shape
    row = lambda c: pl.BlockSpec((tm, D), lambda i, c=c: (i, c))
    return pl.pallas_call(
        _merge_out_kernel,
        grid=(T // tm,),
        in_specs=[row(COL_MERGE_A // D), row(COL_MERGE_B // D), row(0), row(0), row(0),
                  pl.BlockSpec((D, D), lambda i: (0, 0)), pl.BlockSpec((1, D), lambda i: (0, 0))],
        out_specs=[row(0), row(0)],
        out_shape=[jax.ShapeDtypeStruct((T, D), F32), jax.ShapeDtypeStruct((T, D), F32)],
        compiler_params=_params("parallel"),
        name="merge_out",
    )(proj, proj, o_a, o_b, x, w_out, g.reshape(1, D))


def _topk_rows(vals, payload, k):
    n_rows = vals.shape[0]
    row = lax.broadcasted_iota(jnp.int32, vals.shape, 0)
    out_v, out_p = [], []
    for _ in range(k):
        m = jnp.max(vals, axis=0, keepdims=True)
        first = jnp.min(jnp.where(vals == m, row, n_rows), axis=0, keepdims=True)
        hit = row == first
        out_v.append(m)
        out_p.append(jnp.max(jnp.where(hit, payload, -1), axis=0, keepdims=True))
        vals = jnp.where(hit, -jnp.inf, vals)
    return jnp.concatenate(out_v, axis=0), jnp.concatenate(out_p, axis=0)


def _peer_route_kernel(q_ref, k1_ref, k2_ref, e_ref, g_ref):
    tt = q_ref.shape[0]
    key_row = lax.broadcasted_iota(jnp.int32, (PEER_N_KEYS, tt), 0)
    for h in range(PEER_HEADS):
        tops = []
        for c, k_ref in enumerate((k1_ref, k2_ref)):
            qh = q_ref[:, (2 * h + c) * PEER_N_KEYS:(2 * h + c + 1) * PEER_N_KEYS]
            s = _dot_nt(k_ref[...], qh, precision=lax.Precision.HIGHEST)
            tops.append(_topk_rows(s, key_row, PEER_TOPK))
        (v1, i1), (v2, i2) = tops
        cand = jnp.concatenate([v1[a:a + 1, :] + v2 for a in range(PEER_TOPK)], axis=0)
        cand_id = jnp.concatenate([i1[a:a + 1, :] * PEER_N_KEYS + i2 for a in range(PEER_TOPK)], axis=0)
        top_s, top_e = _topk_rows(cand, cand_id, PEER_TOPK)
        ex = jnp.exp(top_s - jnp.max(top_s, axis=0, keepdims=True))
        e_ref[h * PEER_TOPK:(h + 1) * PEER_TOPK, :] = top_e
        g_ref[h * PEER_TOPK:(h + 1) * PEER_TOPK, :] = ex / jnp.sum(ex, axis=0, keepdims=True)


def peer_route(qp, keys1, keys2, tt=256):
    T, Q = qp.shape
    kspec = pl.BlockSpec(keys1.shape, lambda i: (0, 0))
    ospec = pl.BlockSpec((PEER_SLOTS, tt), lambda i: (0, i))
    return pl.pallas_call(
        _peer_route_kernel,
        grid=(T // tt,),
        in_specs=[pl.BlockSpec((tt, Q), lambda i: (i, 0)), kspec, kspec],
        out_specs=[ospec, ospec],
        out_shape=[jax.ShapeDtypeStruct((PEER_SLOTS, T), jnp.int32), jax.ShapeDtypeStruct((PEER_SLOTS, T), F32)],
        compiler_params=_params("parallel"),
        name="peer_route",
    )(qp, keys1, keys2)


LANE = 128
SUBLANE = 8
X_ROWS = D_MODEL // LANE
HALF_ROWS = X_ROWS // 2
UV_WORD_ROWS = 2 * HALF_ROWS
PEER_PITCH = UV_WORD_ROWS + SUBLANE
PEER_NSLOT = 4


def _pack_uv(u, v):
    n_exp = u.shape[0]

    def words(t):
        b = lax.bitcast_convert_type(t.astype(BF16).reshape(n_exp, 2, HALF_ROWS, LANE), jnp.uint16).astype(jnp.uint32)
        return lax.bitcast_convert_type(b[:, 0] | (b[:, 1] << 16), jnp.int32)

    return jnp.concatenate([words(u), words(v)], axis=1)


def _unpack_pair(w):
    lo = lax.bitcast_convert_type(w << 16, F32)
    hi = lax.bitcast_convert_type(w & jnp.int32(-65536), F32)
    return lo, hi


def _peer_apply_kernel(idx_ref, h_ref, gt_ref, uv_hbm, o_ref, *scratch, tb):
    bufs, w_ref, sem = scratch[:PEER_NSLOT], scratch[PEER_NSLOT], scratch[PEER_NSLOT + 1]
    ahead = PEER_NSLOT - 1

    def row_copy(e, j, slot):
        return pltpu.make_async_copy(uv_hbm.at[e], bufs[slot].at[pl.ds(j * PEER_PITCH, UV_WORD_ROWS), :], sem.at[slot])

    def issue(t, slot):
        for j in range(PEER_SLOTS):
            row_copy(idx_ref[t, j], j, slot).start()

    def wait_all(slot):
        n_rows = PEER_SLOTS * UV_WORD_ROWS
        other = bufs[(slot + 1) % PEER_NSLOT]
        pltpu.make_async_copy(other.at[pl.ds(0, n_rows), :], bufs[slot].at[pl.ds(0, n_rows), :], sem.at[slot]).wait()

    lane = lax.broadcasted_iota(jnp.int32, (PEER_SLOTS, tb), 1)

    def compute(t, slot):
        buf = bufs[slot]
        acc = jnp.zeros((PEER_SLOTS, LANE), F32)
        for s in range(HALF_ROWS):
            lo, hi = _unpack_pair(buf[pl.ds(s, PEER_SLOTS, stride=PEER_PITCH), :])
            acc = acc + lo * h_ref[t, s:s + 1, :] + hi * h_ref[t, HALF_ROWS + s:HALF_ROWS + s + 1, :]
        act = _gelu(jnp.sum(acc, axis=1, keepdims=True))
        gate = jnp.sum(jnp.where(lane == t, gt_ref[...], 0.0), axis=1, keepdims=True)
        w_ref[...] = jnp.broadcast_to(gate * act, (PEER_SLOTS, LANE))
        n_acc = 4
        out_lo = [jnp.zeros((SUBLANE, LANE), F32) for _ in range(n_acc)]
        out_hi = [jnp.zeros((SUBLANE, LANE), F32) for _ in range(n_acc)]
        for j in range(PEER_SLOTS):
            wj = w_ref[j:j + 1, :]
            base = j * PEER_PITCH + HALF_ROWS
            lo, hi = _unpack_pair(buf[base:base + HALF_ROWS, :])
            out_lo[j % n_acc] = out_lo[j % n_acc] + lo * wj
            out_hi[j % n_acc] = out_hi[j % n_acc] + hi * wj
        o_ref[t, 0:HALF_ROWS, :] = (out_lo[0] + out_lo[1]) + (out_lo[2] + out_lo[3])
        o_ref[t, HALF_ROWS:X_ROWS, :] = (out_hi[0] + out_hi[1]) + (out_hi[2] + out_hi[3])

    def group(g, carry):
        t0 = g * PEER_NSLOT
        for i in range(PEER_NSLOT):
            wait_all(i)
            issue(jnp.minimum(t0 + i + ahead, tb - 1), (i + ahead) % PEER_NSLOT)
            compute(t0 + i, i)
        return carry

    for t in range(ahead):
        issue(t, t)
    lax.fori_loop(0, tb // PEER_NSLOT, group, 0)
    for t in range(ahead):
        wait_all((tb + t) % PEER_NSLOT)


def peer_apply(experts, h, gate_t, uv, tb=128):
    T, D = h.shape
    tok_spec = pl.BlockSpec((tb, X_ROWS, LANE), lambda i: (i, 0, 0))
    y = pl.pallas_call(
        functools.partial(_peer_apply_kernel, tb=tb),
        grid=(T // tb,),
        in_specs=[
            pl.BlockSpec((tb, PEER_SLOTS), lambda i: (i, 0), memory_space=pltpu.SMEM),
            tok_spec,
            pl.BlockSpec((PEER_SLOTS, tb), lambda i: (0, i)),
            pl.BlockSpec(memory_space=pl.ANY),
        ],
        out_specs=tok_spec,
        out_shape=jax.ShapeDtypeStruct((T, X_ROWS, LANE), F32),
        scratch_shapes=[pltpu.VMEM((PEER_SLOTS * PEER_PITCH, LANE), jnp.int32) for _ in range(PEER_NSLOT)] + [
            pltpu.VMEM((PEER_SLOTS, LANE), F32), pltpu.SemaphoreType.DMA((PEER_NSLOT,))],
        compiler_params=_params("arbitrary"),
        name="peer_apply",
    )(experts, h.reshape(T, X_ROWS, LANE), gate_t, uv)
    return y.reshape(T, D)


def _final_kernel(x_ref, y_ref, g_ref, o_ref):
    o_ref[...] = _rms(x_ref[...] + y_ref[...], g_ref[...])


def final_norm(x1, y, g, tm=512):
    T, D = x1.shape
    spec = pl.BlockSpec((tm, D), lambda i: (i, 0))
    return pl.pallas_call(
        _final_kernel,
        grid=(T // tm,),
        in_specs=[spec, spec, pl.BlockSpec((1, D), lambda i: (0, 0))],
        out_specs=spec,
        out_shape=jax.ShapeDtypeStruct((T, D), F32),
        compiler_params=_params("parallel"),
        name="final_norm",
    )(x1, y, g.reshape(1, D))


def _pack_w_in(w):
    u, v, q, kv, gate, merge = jnp.split(w, [2048, 4096, 6144, 9216, 9264], axis=1)
    gate = gate.reshape(D_MODEL, NSA_KV_HEADS, NSA_GROUP * 3)
    gate = jnp.pad(gate, ((0, 0), (0, 0), (0, GATE_PAD - NSA_GROUP * 3))).reshape(D_MODEL, NSA_KV_HEADS * GATE_PAD)
    return jnp.concatenate([u, v, q, merge, kv, gate], axis=1).astype(BF16)


def _rope_tables(S):
    half = HEAD_DIM // 2
    inv = ROPE_THETA ** (-jnp.arange(half, dtype=F32) / half)
    ang = jnp.arange(S, dtype=F32)[:, None] * inv[None, :]
    cos = jnp.cos(ang)
    sin = jnp.sin(ang)
    return jnp.concatenate([cos, cos], axis=1), jnp.concatenate([-sin, sin], axis=1)


def _layer(x, norm_mix_g, w_in, w_out, gm_ln_g, gm_ln_b, gm_spatial_w, gm_spatial_b,
           cmp_pos_k, cmp_w1_k, cmp_w2_k, cmp_pos_v, cmp_w1_v, cmp_w2_v,
           norm_ffn_g, peer_w_q, peer_keys1, peer_keys2, peer_u, peer_v):
    B, S, D = x.shape
    T = B * S
    xt = x.reshape(T, D)
    h = rmsnorm_cast(xt, norm_mix_g)
    proj = matmul(h, _pack_w_in(w_in), 1024, 1536, F32, "in_proj")
    o_a = gmlp_mixer(proj, gm_ln_g, gm_ln_b, gm_spatial_w, gm_spatial_b)
    proj3 = proj.reshape(B, S, COL_TOTAL)
    cos, sin = _rope_tables(S)
    kc, vc, ks, vs, kw, vw = nsa_prep(proj3, cos, sin, cmp_pos_k, cmp_w1_k, cmp_w2_k, cmp_pos_v, cmp_w1_v, cmp_w2_v)
    o_b = nsa_attention(proj3, proj3, kc, vc, ks, vs, kw, vw, cos, sin).reshape(T, D)
    x1, h2 = merge_out(proj, o_a, o_b, xt, w_out.astype(BF16), norm_ffn_g)
    qp = matmul(h2.astype(BF16), peer_w_q.astype(BF16), 1024, 1024, F32, "peer_query")
    experts_t, gate_t = peer_route(qp, peer_keys1, peer_keys2)
    y = peer_apply(experts_t.T, h2, gate_t, _pack_uv(peer_u, peer_v))
    return x1, y


def kernel(x, norm_mix_g, w_in, w_out, gm_ln_g, gm_ln_b, gm_spatial_w, gm_spatial_b, cmp_pos_k, cmp_w1_k, cmp_w2_k, cmp_pos_v, cmp_w1_v, cmp_w2_v, norm_ffn_g, peer_w_q, peer_keys1, peer_keys2, peer_u, peer_v, norm_final_g):
    B, S, D = x.shape
    depth = w_in.shape[0]
    y = None
    for l in range(depth):
        if y is not None:
            x = (x.reshape(B * S, D) + y).reshape(B, S, D)
        x1, y = _layer(x, norm_mix_g[l], w_in[l], w_out[l], gm_ln_g[l], gm_ln_b[l], gm_spatial_w[l], gm_spatial_b[l],
                       cmp_pos_k[l], cmp_w1_k[l], cmp_w2_k[l], cmp_pos_v[l], cmp_w1_v[l], cmp_w2_v[l],
                       norm_ffn_g[l], peer_w_q[l], peer_keys1[l], peer_keys2[l], peer_u[l], peer_v[l])
        x = x1.reshape(B, S, D)
    return final_norm(x.reshape(B * S, D), y, norm_final_g).reshape(B, S, D)
```

```python
import functools
import math

import numpy as np
import jax
import jax.numpy as jnp
from jax import lax
from jax.experimental import pallas as pl
from jax.experimental.pallas import tpu as pltpu

D_MODEL = 2048
GM_GROUPS = 16
GM_GROUP_DIM = 128
GM_CHUNK = 128

NSA_HEADS = 16
NSA_KV_HEADS = 4
NSA_GROUP = 4
HEAD_DIM = 128
CMP_BLOCK = 32
CMP_STRIDE = 16
CMP_HIDDEN = 256
SLC_BLOCK = 64
N_SELECT = 16
WINDOW = 512
ROPE_THETA = 10000.0

PEER_HEADS = 8
PEER_N_KEYS = 128
PEER_TOPK = 16
PEER_SLOTS = PEER_HEADS * PEER_TOPK

EPS = 1e-6
NEG_BIG = -1e30
POS_BIG = 1e30

VMEM_LIMIT_BYTES = 56 * 1024 * 1024
LANE = 128
SUBLANE = 8

COL_U = 0
COL_V = 2048
COL_Q = 4096
COL_MERGE_A = 6144
COL_MERGE_B = 8192
COL_KV = 10240
COL_GATE = 13312
GATE_PAD = 128
COL_TOTAL = COL_GATE + NSA_KV_HEADS * GATE_PAD

BF16 = jnp.bfloat16
F32 = jnp.float32


def _params(*sem):
    return pltpu.CompilerParams(dimension_semantics=sem, vmem_limit_bytes=VMEM_LIMIT_BYTES)


def _gelu(x):
    return 0.5 * x * (1.0 + jnp.tanh(math.sqrt(2.0 / math.pi) * (x + 0.044715 * (x * x * x))))


def _sigmoid(x):
    return 1.0 / (1.0 + jnp.exp(-x))


def _rms(x, g):
    return x * lax.rsqrt(jnp.mean(x * x, axis=-1, keepdims=True) + EPS) * g


def _dot_nt(a, b, **kw):
    return lax.dot_general(a, b, (((1,), (1,)), ((), ())), preferred_element_type=F32, **kw)


def _rmsnorm_kernel(x_ref, g_ref, o_ref):
    o_ref[...] = _rms(x_ref[...], g_ref[...]).astype(o_ref.dtype)


def rmsnorm_cast(x, g, tm=512):
    T, D = x.shape
    return pl.pallas_call(
        _rmsnorm_kernel,
        grid=(T // tm,),
        in_specs=[pl.BlockSpec((tm, D), lambda i: (i, 0)), pl.BlockSpec((1, D), lambda i: (0, 0))],
        out_specs=pl.BlockSpec((tm, D), lambda i: (i, 0)),
        out_shape=jax.ShapeDtypeStruct((T, D), BF16),
        compiler_params=_params("parallel"),
        name="rmsnorm_cast",
    )(x, g.reshape(1, D))


def _mm_kernel(a_ref, w_ref, o_ref):
    o_ref[...] = jnp.dot(a_ref[...], w_ref[...], preferred_element_type=F32).astype(o_ref.dtype)


def matmul(a, w, tm, tn, out_dtype, name):
    M, K = a.shape
    N = w.shape[1]
    return pl.pallas_call(
        _mm_kernel,
        grid=(N // tn, M // tm),
        in_specs=[pl.BlockSpec((tm, K), lambda j, i: (i, 0)), pl.BlockSpec((K, tn), lambda j, i: (0, j))],
        out_specs=pl.BlockSpec((tm, tn), lambda j, i: (i, j)),
        out_shape=jax.ShapeDtypeStruct((M, N), out_dtype),
        compiler_params=_params("parallel", "parallel"),
        name=name,
    )(a, w)


def _gmlp_kernel(u_ref, v_ref, g_ref, b_ref, w_ref, bs_ref, o_ref):
    gv = _gelu(v_ref[...])
    mu = jnp.mean(gv, axis=-1, keepdims=True)
    xc = gv - mu
    vn = xc * lax.rsqrt(jnp.mean(xc * xc, axis=-1, keepdims=True) + EPS) * g_ref[...] + b_ref[...]
    vn = vn.astype(BF16)
    row = lax.broadcasted_iota(jnp.int32, (GM_CHUNK, GM_CHUNK), 0)
    col = lax.broadcasted_iota(jnp.int32, (GM_CHUNK, GM_CHUNK), 1)
    causal = row >= col
    bs = bs_ref[...]
    for g in range(GM_GROUPS):
        sl = slice(g * GM_GROUP_DIM, (g + 1) * GM_GROUP_DIM)
        w = jnp.where(causal, w_ref[g], 0.0).astype(BF16)
        z = jnp.dot(w, vn[:, sl], preferred_element_type=F32) + bs[:, g:g + 1]
        o_ref[:, sl] = (_gelu(u_ref[:, sl]) * z).astype(o_ref.dtype)


def gmlp_mixer(proj, ln_g, ln_b, w_s, b_s):
    T = proj.shape[0]
    W = GM_GROUPS * GM_GROUP_DIM
    return pl.pallas_call(
        _gmlp_kernel,
        grid=(T // GM_CHUNK,),
        in_specs=[
            pl.BlockSpec((GM_CHUNK, W), lambda i: (i, COL_U // W)),
            pl.BlockSpec((GM_CHUNK, W), lambda i: (i, COL_V // W)),
            pl.BlockSpec((1, W), lambda i: (0, 0)),
            pl.BlockSpec((1, W), lambda i: (0, 0)),
            pl.BlockSpec((GM_GROUPS, GM_CHUNK, GM_CHUNK), lambda i: (0, 0, 0)),
            pl.BlockSpec((GM_CHUNK, GM_GROUPS), lambda i: (0, 0)),
        ],
        out_specs=pl.BlockSpec((GM_CHUNK, W), lambda i: (i, 0)),
        out_shape=jax.ShapeDtypeStruct((T, W), BF16),
        compiler_params=_params("parallel"),
        name="gmlp_mixer",
    )(proj, proj, ln_g.reshape(1, W), ln_b.reshape(1, W), w_s, b_s.T)


def _rope(x, cos, sin_signed):
    return x * cos + pltpu.roll(x, HEAD_DIM // 2, axis=1) * sin_signed


def _compress(k_ref, pos_ref, w1_ref, w2_ref, nc):
    half = CMP_BLOCK // 2
    a = jnp.zeros((nc, CMP_HIDDEN), F32)
    b = jnp.zeros((nc, CMP_HIDDEN), F32)
    for l in range(half):
        kl = k_ref[0, pl.ds(l, nc, stride=CMP_STRIDE), :]
        a = a + jnp.dot((kl + pos_ref[l:l + 1, :]).astype(BF16), w1_ref[l], preferred_element_type=F32)
        b = b + jnp.dot((kl + pos_ref[half + l:half + l + 1, :]).astype(BF16), w1_ref[half + l],
                        preferred_element_type=F32)
    hid = a + pltpu.roll(b, nc - 1, axis=0)
    return jnp.dot(_gelu(hid).astype(BF16), w2_ref[...], preferred_element_type=F32)


def _nsa_prep_kernel(kc_ref, vc_ref, ks_ref, vs_ref, kw_ref, vw_ref, cos_ref, sin_ref,
                     pk_ref, w1k_ref, w2k_ref, pv_ref, w1v_ref, w2v_ref,
                     okc_ref, ovc_ref, oks_ref, ovs_ref, okw_ref, ovw_ref, *, nc):
    okc_ref[0, 0] = _compress(kc_ref, pk_ref, w1k_ref, w2k_ref, nc).astype(okc_ref.dtype)
    ovc_ref[0, 0] = _compress(vc_ref, pv_ref, w1v_ref, w2v_ref, nc).astype(ovc_ref.dtype)
    cos = cos_ref[...]
    sin = sin_ref[...]
    oks_ref[0, 0] = _rope(ks_ref[0], cos, sin).astype(oks_ref.dtype)
    okw_ref[0, 0] = _rope(kw_ref[0], cos, sin).astype(okw_ref.dtype)
    ovs_ref[0, 0] = vs_ref[0].astype(ovs_ref.dtype)
    ovw_ref[0, 0] = vw_ref[0].astype(ovw_ref.dtype)


def nsa_prep(proj3, cos, sin, pos_k, w1_k, w2_k, pos_v, w1_v, w2_v):
    B, S, _ = proj3.shape
    nc = S // CMP_STRIDE
    kvb = COL_KV // HEAD_DIM

    def kv_spec(i):
        return pl.BlockSpec((1, S, HEAD_DIM), lambda b, h, i=i: (b, 0, kvb + i * NSA_KV_HEADS + h))

    full2 = lambda shp: pl.BlockSpec(shp, lambda b, h: (0, 0))
    full3 = lambda shp: pl.BlockSpec(shp, lambda b, h: (0, 0, 0))
    out_c = pl.BlockSpec((1, 1, nc, HEAD_DIM), lambda b, h: (b, h, 0, 0))
    out_s = pl.BlockSpec((1, 1, S, HEAD_DIM), lambda b, h: (b, h, 0, 0))
    shp_c = jax.ShapeDtypeStruct((B, NSA_KV_HEADS, nc, HEAD_DIM), BF16)
    shp_s = jax.ShapeDtypeStruct((B, NSA_KV_HEADS, S, HEAD_DIM), BF16)
    w1k = w1_k.astype(BF16).reshape(CMP_BLOCK, HEAD_DIM, CMP_HIDDEN)
    w1v = w1_v.astype(BF16).reshape(CMP_BLOCK, HEAD_DIM, CMP_HIDDEN)
    return pl.pallas_call(
        functools.partial(_nsa_prep_kernel, nc=nc),
        grid=(B, NSA_KV_HEADS),
        in_specs=[kv_spec(i) for i in range(6)] + [
            full2((S, HEAD_DIM)), full2((S, HEAD_DIM)),
            full2((CMP_BLOCK, HEAD_DIM)), full3((CMP_BLOCK, HEAD_DIM, CMP_HIDDEN)), full2((CMP_HIDDEN, HEAD_DIM)),
            full2((CMP_BLOCK, HEAD_DIM)), full3((CMP_BLOCK, HEAD_DIM, CMP_HIDDEN)), full2((CMP_HIDDEN, HEAD_DIM)),
        ],
        out_specs=[out_c, out_c, out_s, out_s, out_s, out_s],
        out_shape=[shp_c, shp_c, shp_s, shp_s, shp_s, shp_s],
        compiler_params=_params("parallel", "parallel"),
        name="nsa_prep",
    )(proj3, proj3, proj3, proj3, proj3, proj3, cos, sin,
      pos_k, w1k, w2_k.astype(BF16), pos_v, w1v, w2_v.astype(BF16))


def _softmax_rows(s, mask):
    s = jnp.where(mask, s, NEG_BIG)
    m = jnp.max(s, axis=-1, keepdims=True)
    e = jnp.exp(s - m)
    return e / jnp.sum(e, axis=-1, keepdims=True)


def _nsa_attn_kernel(q_ref, gate_ref, kc_ref, vc_ref, ks_ref, vs_ref, kw_ref, vw_ref,
                     cos_ref, sin_ref, agg_ref, exp_ref, o_ref, mask_ref, m_ref, l_ref, acc_ref, *, tq, S, n_sel, wk):
    qt = pl.program_id(2)
    nc = S // CMP_STRIDE
    n_slc = S // SLC_BLOCK
    scale = HEAD_DIM ** -0.5
    pos = qt * tq + lax.broadcasted_iota(jnp.int32, (tq, 1), 0)
    cos = cos_ref[...]
    sin = sin_ref[...]
    gates = _sigmoid(gate_ref[0])

    qs = [q_ref[0, :, g * HEAD_DIM:(g + 1) * HEAD_DIM] for g in range(NSA_GROUP)]
    qr = [_rope(q, cos, sin).astype(BF16) for q in qs]

    kc = kc_ref[0, 0]
    vc = vc_ref[0, 0]
    cmp_end = lax.broadcasted_iota(jnp.int32, (1, nc), 1) * CMP_STRIDE + (CMP_BLOCK - 1)
    cmask = (cmp_end <= pos) & (cmp_end < S)
    any_c = pos >= CMP_BLOCK - 1
    o_cmp = []
    imp = jnp.zeros((tq, nc), F32)
    for g in range(NSA_GROUP):
        s = _dot_nt(qs[g].astype(BF16), kc) * scale
        p = jnp.where(any_c, _softmax_rows(s, cmask), 0.0)
        o_cmp.append(jnp.dot(p.astype(BF16), vc, preferred_element_type=F32))
        imp = imp + p

    imp_slc = jnp.dot(imp, agg_ref[...], preferred_element_type=F32, precision=lax.Precision.HIGHEST)
    blk = lax.broadcasted_iota(jnp.int32, (1, n_slc), 1)
    allowed = blk * SLC_BLOCK <= pos
    forced = (blk == 0) | (blk == pos // SLC_BLOCK)
    score = jnp.where(forced, POS_BIG, jnp.where(allowed, imp_slc, NEG_BIG))
    rank = jnp.zeros((tq, n_slc), F32)
    for i in range(n_slc):
        si = score[:, i:i + 1]
        earlier = (blk > i).astype(F32)
        rank = rank + jnp.where(si > score, 1.0, jnp.where(si == score, earlier, 0.0))
    sel = jnp.where(rank < n_sel, 1.0, 0.0).astype(BF16)

    for c in range(S // tq):
        in_sel = jnp.dot(sel, exp_ref[:, c * tq:(c + 1) * tq], preferred_element_type=F32) > 0.5
        kpos = c * tq + lax.broadcasted_iota(jnp.int32, (1, tq), 1)
        mask_ref[c] = jnp.where(in_sel & (kpos <= pos), 1.0, 0.0)
    m_ref[...] = jnp.full(m_ref.shape, NEG_BIG, F32)
    l_ref[...] = jnp.zeros(l_ref.shape, F32)
    acc_ref[...] = jnp.zeros(acc_ref.shape, F32)

    def slc_chunk(c, carry):
        off = pl.multiple_of(c * tq, tq)
        k_c = ks_ref[0, 0, pl.ds(off, tq), :]
        v_c = vs_ref[0, 0, pl.ds(off, tq), :]
        valid = mask_ref[c] > 0.5
        for g in range(NSA_GROUP):
            s = jnp.where(valid, _dot_nt(qr[g], k_c) * scale, NEG_BIG)
            m_old = m_ref[g]
            m_new = jnp.maximum(m_old, jnp.max(s, axis=-1, keepdims=True))
            alpha = jnp.exp(m_old - m_new)
            p = jnp.exp(s - jnp.concatenate([m_new] * (tq // LANE), axis=1))
            l_ref[g] = alpha * l_ref[g] + jnp.sum(p, axis=-1, keepdims=True)
            acc_ref[g] = alpha * acc_ref[g] + jnp.dot(p.astype(BF16), v_c, preferred_element_type=F32)
            m_ref[g] = m_new
        return carry

    lax.fori_loop(0, qt + 1, slc_chunk, 0)
    o_slc = [acc_ref[g] / l_ref[g] for g in range(NSA_GROUP)]

    start = pl.multiple_of(jnp.maximum(qt * tq - (wk - tq), 0), tq)
    kw = kw_ref[0, 0, pl.ds(start, wk), :]
    vw = vw_ref[0, 0, pl.ds(start, wk), :]
    diff = pos - (start + lax.broadcasted_iota(jnp.int32, (1, wk), 1))
    wmask = (diff >= 0) & (diff < WINDOW)
    for g in range(NSA_GROUP):
        s = _dot_nt(qr[g], kw) * scale
        p = _softmax_rows(s, wmask)
        o_win = jnp.dot(p.astype(BF16), vw, preferred_element_type=F32)
        c = g * 3
        o = gates[:, c:c + 1] * o_cmp[g] + gates[:, c + 1:c + 2] * o_slc[g] + gates[:, c + 2:c + 3] * o_win
        o_ref[0, :, g * HEAD_DIM:(g + 1) * HEAD_DIM] = o.astype(o_ref.dtype)


def _selection_constants(S):
    nc = S // CMP_STRIDE
    n_cmp = (S - CMP_BLOCK) // CMP_STRIDE + 1
    n_slc = S // SLC_BLOCK
    ratio = SLC_BLOCK // CMP_STRIDE
    span = CMP_BLOCK // CMP_STRIDE
    agg_w = np.convolve(np.ones(ratio), np.ones(span))
    agg = np.zeros((nc, n_slc), np.float32)
    for j in range(n_slc):
        for w in range(ratio + span - 1):
            c = ratio * j + w - (span - 1)
            if 0 <= c < n_cmp:
                agg[c, j] += agg_w[w]
    expand = (np.arange(S)[None, :] // SLC_BLOCK == np.arange(n_slc)[:, None]).astype(np.float32)
    return jnp.asarray(agg), jnp.asarray(expand, dtype=BF16)


def nsa_attention(proj3, gate3, kc, vc, ks, vs, kw, vw, cos, sin, tq=256):
    B, S, _ = proj3.shape
    nc = S // CMP_STRIDE
    n_slc = S // SLC_BLOCK
    n_sel = min(N_SELECT, n_slc)
    wk = min(S, WINDOW + tq)
    agg, expand = _selection_constants(S)
    qw = NSA_GROUP * HEAD_DIM
    cspec = pl.BlockSpec((1, 1, nc, HEAD_DIM), lambda b, h, t: (b, h, 0, 0))
    sspec = pl.BlockSpec((1, 1, S, HEAD_DIM), lambda b, h, t: (b, h, 0, 0))
    return pl.pallas_call(
        functools.partial(_nsa_attn_kernel, tq=tq, S=S, n_sel=n_sel, wk=wk),
        grid=(B, NSA_KV_HEADS, S // tq),
        in_specs=[
            pl.BlockSpec((1, tq, qw), lambda b, h, t: (b, t, COL_Q // qw + h)),
            pl.BlockSpec((1, tq, GATE_PAD), lambda b, h, t: (b, t, COL_GATE // GATE_PAD + h)),
            cspec, cspec, sspec, sspec, sspec, sspec,
            pl.BlockSpec((tq, HEAD_DIM), lambda b, h, t: (t, 0)),
            pl.BlockSpec((tq, HEAD_DIM), lambda b, h, t: (t, 0)),
            pl.BlockSpec((nc, n_slc), lambda b, h, t: (0, 0)),
            pl.BlockSpec((n_slc, S), lambda b, h, t: (0, 0)),
        ],
        out_specs=pl.BlockSpec((1, tq, qw), lambda b, h, t: (b, t, h)),
        out_shape=jax.ShapeDtypeStruct((B, S, NSA_HEADS * HEAD_DIM), BF16),
        scratch_shapes=[pltpu.VMEM((S // tq, tq, tq), F32)] + [
            pltpu.VMEM((NSA_GROUP, tq, HEAD_DIM), F32) for _ in range(3)],
        compiler_params=_params("parallel", "parallel", "parallel"),
        name="nsa_attention",
    )(proj3, gate3, kc, vc, ks, vs, kw, vw, cos, sin, agg, expand)


def _merge_out_kernel(ma_ref, mb_ref, oa_ref, ob_ref, x_ref, w_ref, g_ref, x1_ref, h_ref):
    m = _sigmoid(ma_ref[...]) * oa_ref[...].astype(F32) + _sigmoid(mb_ref[...]) * ob_ref[...].astype(F32)
    x1 = x_ref[...] + jnp.dot(m.astype(BF16), w_ref[...], preferred_element_type=F32)
    x1_ref[...] = x1
    h_ref[...] = _rms(x1, g_ref[...])


def merge_out(proj, o_a, o_b, x, w_out, g, tm=256):
    T, D = x.shape
    row = lambda c: pl.BlockSpec((tm, D), lambda i, c=c: (i, c))
    return pl.pallas_call(
        _merge_out_kernel,
        grid=(T // tm,),
        in_specs=[row(COL_MERGE_A // D), row(COL_MERGE_B // D), row(0), row(0), row(0),
                  pl.BlockSpec((D, D), lambda i: (0, 0)), pl.BlockSpec((1, D), lambda i: (0, 0))],
        out_specs=[row(0), row(0)],
        out_shape=[jax.ShapeDtypeStruct((T, D), F32), jax.ShapeDtypeStruct((T, D), F32)],
        compiler_params=_params("parallel"),
        name="merge_out",
    )(proj, proj, o_a, o_b, x, w_out, g.reshape(1, D))


def _topk_rows(vals, payload, k):
    n_rows = vals.shape[0]
    row = lax.broadcasted_iota(jnp.int32, vals.shape, 0)
    out_v, out_p = [], []
    for _ in range(k):
        m = jnp.max(vals, axis=0, keepdims=True)
        first = jnp.min(jnp.where(vals == m, row, n_rows), axis=0, keepdims=True)
        hit = row == first
        out_v.append(m)
        out_p.append(jnp.max(jnp.where(hit, payload, -1), axis=0, keepdims=True))
        vals = jnp.where(hit, -jnp.inf, vals)
    return jnp.concatenate(out_v, axis=0), jnp.concatenate(out_p, axis=0)


def _peer_route_kernel(q_ref, k1_ref, k2_ref, e_ref, g_ref):
    tt = q_ref.shape[0]
    key_row = lax.broadcasted_iota(jnp.int32, (PEER_N_KEYS, tt), 0)
    for h in range(PEER_HEADS):
        tops = []
        for c, k_ref in enumerate((k1_ref, k2_ref)):
            qh = q_ref[:, (2 * h + c) * PEER_N_KEYS:(2 * h + c + 1) * PEER_N_KEYS]
            s = _dot_nt(k_ref[...], qh, precision=lax.Precision.HIGHEST)
            tops.append(_topk_rows(s, key_row, PEER_TOPK))
        (v1, i1), (v2, i2) = tops
        cand = jnp.concatenate([v1[a:a + 1, :] + v2 for a in range(PEER_TOPK)], axis=0)
        cand_id = jnp.concatenate([i1[a:a + 1, :] * PEER_N_KEYS + i2 for a in range(PEER_TOPK)], axis=0)
        top_s, top_e = _topk_rows(cand, cand_id, PEER_TOPK)
        ex = jnp.exp(top_s - jnp.max(top_s, axis=0, keepdims=True))
        e_ref[h * PEER_TOPK:(h + 1) * PEER_TOPK, :] = top_e
        g_ref[h * PEER_TOPK:(h + 1) * PEER_TOPK, :] = ex / jnp.sum(ex, axis=0, keepdims=True)


def peer_route(qp, keys1, keys2, tt=256):
    T, Q = qp.shape
    kspec = pl.BlockSpec(keys1.shape, lambda i: (0, 0))
    ospec = pl.BlockSpec((PEER_SLOTS, tt), lambda i: (0, i))
    return pl.pallas_call(
        _peer_route_kernel,
        grid=(T // tt,),
        in_specs=[pl.BlockSpec((tt, Q), lambda i: (i, 0)), kspec, kspec],
        out_specs=[ospec, ospec],
        out_shape=[jax.ShapeDtypeStruct((PEER_SLOTS, T), jnp.int32), jax.ShapeDtypeStruct((PEER_SLOTS, T), F32)],
        compiler_params=_params("parallel"),
        name="peer_route",
    )(qp, keys1, keys2)


X_ROWS = D_MODEL // LANE
UV_ROWS = 2 * X_ROWS
PEER_NSLOT = 4
DMA_PRIORITIES = 2


def _pack_uv(u, v):
    n_exp = u.shape[0]
    return jnp.concatenate([u.reshape(n_exp, X_ROWS, LANE), v.reshape(n_exp, X_ROWS, LANE)], axis=1).astype(BF16)


def _peer_apply_kernel(idx_ref, idx_next_ref, h_ref, gt_ref, uv_hbm, o_ref, *scratch, tb):
    bufs, p_ref, w_ref, sem = scratch[:PEER_NSLOT], scratch[PEER_NSLOT], scratch[PEER_NSLOT + 1], scratch[PEER_NSLOT + 2]
    ahead = PEER_NSLOT - 1
    step = pl.program_id(0)

    def row_copy(ids, t, j, slot, row):
        return pltpu.make_async_copy(uv_hbm.at[ids[t, j]], bufs[slot].at[pl.ds(row, UV_ROWS), :], sem.at[slot])

    def issue(ids, t, slot):
        for j in range(PEER_SLOTS):
            row_copy(ids, t, j, slot, j * UV_ROWS).start(priority=j % DMA_PRIORITIES)

    def issue_rolled(ids, t, slot):
        def body(j, carry):
            row_copy(ids, t, j, slot, pl.multiple_of(j * UV_ROWS, UV_ROWS)).start()
            return carry
        lax.fori_loop(0, PEER_SLOTS, body, 0)

    def wait_all(slot):
        pltpu.make_async_copy(bufs[(slot + 1) % PEER_NSLOT], bufs[slot], sem.at[slot]).wait()

    lane = lax.broadcasted_iota(jnp.int32, (PEER_SLOTS, tb), 1)

    def compute(t, slot):
        buf = bufs[slot]
        x = h_ref[t]
        for j in range(PEER_SLOTS):
            prod = buf[j * UV_ROWS:j * UV_ROWS + X_ROWS, :].astype(F32) * x
            p_ref[j * SUBLANE:(j + 1) * SUBLANE, :] = prod[:SUBLANE] + prod[SUBLANE:]
        acc = p_ref[pl.ds(0, PEER_SLOTS, stride=SUBLANE), :]
        for q in range(1, SUBLANE):
            acc = acc + p_ref[pl.ds(q, PEER_SLOTS, stride=SUBLANE), :]
        act = _gelu(jnp.sum(acc, axis=1, keepdims=True))
        gate = jnp.sum(jnp.where(lane == t, gt_ref[...], 0.0), axis=1, keepdims=True)
        w_ref[...] = jnp.broadcast_to(gate * act, (PEER_SLOTS, LANE))
        n_acc = 4
        out = [jnp.zeros((X_ROWS, LANE), F32) for _ in range(n_acc)]
        for j in range(PEER_SLOTS):
            base = j * UV_ROWS + X_ROWS
            out[j % n_acc] = out[j % n_acc] + buf[base:base + X_ROWS, :].astype(F32) * w_ref[j:j + 1, :]
        o_ref[t] = (out[0] + out[1]) + (out[2] + out[3])

    def group(g, carry):
        t0 = g * PEER_NSLOT
        for i in range(PEER_NSLOT):
            wait_all(i)
            issue(idx_ref, t0 + i + ahead, (i + ahead) % PEER_NSLOT)
            compute(t0 + i, i)
        return carry

    @pl.when(step == 0)
    def _():
        for t in range(ahead):
            issue_rolled(idx_ref, t, t)

    n_groups = tb // PEER_NSLOT
    lax.fori_loop(0, n_groups - 1, group, 0)
    t0 = (n_groups - 1) * PEER_NSLOT
    for i in range(PEER_NSLOT):
        wait_all(i)
        if i + ahead < PEER_NSLOT:
            issue(idx_ref, t0 + i + ahead, (i + ahead) % PEER_NSLOT)
        else:
            @pl.when(step + 1 < pl.num_programs(0))
            def _(i=i):
                issue(idx_next_ref, i + ahead - PEER_NSLOT, (i + ahead) % PEER_NSLOT)
        compute(t0 + i, i)


def peer_apply(experts, h, gate_t, uv, tb=64):
    T, D = h.shape
    n_blocks = T // tb
    tok_spec = pl.BlockSpec((tb, X_ROWS, LANE), lambda i: (i, 0, 0))
    gate_blocks = gate_t.reshape(PEER_SLOTS, n_blocks, tb).transpose(1, 0, 2)
    y = pl.pallas_call(
        functools.partial(_peer_apply_kernel, tb=tb),
        grid=(n_blocks,),
        in_specs=[
            pl.BlockSpec((tb, PEER_SLOTS), lambda i: (i, 0), memory_space=pltpu.SMEM),
            pl.BlockSpec((tb, PEER_SLOTS), lambda i: (jnp.minimum(i + 1, n_blocks - 1), 0), memory_space=pltpu.SMEM),
            tok_spec,
            pl.BlockSpec((None, PEER_SLOTS, tb), lambda i: (i, 0, 0)),
            pl.BlockSpec(memory_space=pl.ANY),
        ],
        out_specs=tok_spec,
        out_shape=jax.ShapeDtypeStruct((T, X_ROWS, LANE), F32),
        scratch_shapes=[pltpu.VMEM((PEER_SLOTS * UV_ROWS, LANE), BF16) for _ in range(PEER_NSLOT)] + [
            pltpu.VMEM((PEER_SLOTS * SUBLANE, LANE), F32), pltpu.VMEM((PEER_SLOTS, LANE), F32),
            pltpu.SemaphoreType.DMA((PEER_NSLOT,))],
        compiler_params=_params("arbitrary"),
        name="peer_apply",
    )(experts, experts, h.reshape(T, X_ROWS, LANE), gate_blocks, uv)
    return y.reshape(T, D)


def _final_kernel(x_ref, y_ref, g_ref, o_ref):
    o_ref[...] = _rms(x_ref[...] + y_ref[...], g_ref[...])


def final_norm(x1, y, g, tm=512):
    T, D = x1.shape
    spec = pl.BlockSpec((tm, D), lambda i: (i, 0))
    return pl.pallas_call(
        _final_kernel,
        grid=(T // tm,),
        in_specs=[spec, spec, pl.BlockSpec((1, D), lambda i: (0, 0))],
        out_specs=spec,
        out_shape=jax.ShapeDtypeStruct((T, D), F32),
        compiler_params=_params("parallel"),
        name="final_norm",
    )(x1, y, g.reshape(1, D))


def _pack_w_in(w):
    u, v, q, kv, gate, merge = jnp.split(w, [2048, 4096, 6144, 9216, 9264], axis=1)
    gate = gate.reshape(D_MODEL, NSA_KV_HEADS, NSA_GROUP * 3)
    gate = jnp.pad(gate, ((0, 0), (0, 0), (0, GATE_PAD - NSA_GROUP * 3))).reshape(D_MODEL, NSA_KV_HEADS * GATE_PAD)
    return jnp.concatenate([u, v, q, merge, kv, gate], axis=1).astype(BF16)


def _rope_tables(S):
    half = HEAD_DIM // 2
    inv = ROPE_THETA ** (-jnp.arange(half, dtype=F32) / half)
    ang = jnp.arange(S, dtype=F32)[:, None] * inv[None, :]
    cos = jnp.cos(ang)
    sin = jnp.sin(ang)
    return jnp.concatenate([cos, cos], axis=1), jnp.concatenate([-sin, sin], axis=1)


def _layer(x, norm_mix_g, w_in, w_out, gm_ln_g, gm_ln_b, gm_spatial_w, gm_spatial_b,
           cmp_pos_k, cmp_w1_k, cmp_w2_k, cmp_pos_v, cmp_w1_v, cmp_w2_v,
           norm_ffn_g, peer_w_q, peer_keys1, peer_keys2, peer_u, peer_v):
    B, S, D = x.shape
    T = B * S
    xt = x.reshape(T, D)
    h = rmsnorm_cast(xt, norm_mix_g)
    proj = matmul(h, _pack_w_in(w_in), 1024, 1536, F32, "in_proj")
    o_a = gmlp_mixer(proj, gm_ln_g, gm_ln_b, gm_spatial_w, gm_spatial_b)
    proj3 = proj.reshape(B, S, COL_TOTAL)
    cos, sin = _rope_tables(S)
    kc, vc, ks, vs, kw, vw = nsa_prep(proj3, cos, sin, cmp_pos_k, cmp_w1_k, cmp_w2_k, cmp_pos_v, cmp_w1_v, cmp_w2_v)
    o_b = nsa_attention(proj3, proj3, kc, vc, ks, vs, kw, vw, cos, sin).reshape(T, D)
    x1, h2 = merge_out(proj, o_a, o_b, xt, w_out.astype(BF16), norm_ffn_g)
    qp = matmul(h2.astype(BF16), peer_w_q.astype(BF16), 1024, 1024, F32, "peer_query")
    experts_t, gate_t = peer_route(qp, peer_keys1, peer_keys2)
    y = peer_apply(experts_t.T, h2, gate_t, _pack_uv(peer_u, peer_v))
    return x1, y


def kernel(x, norm_mix_g, w_in, w_out, gm_ln_g, gm_ln_b, gm_spatial_w, gm_spatial_b, cmp_pos_k, cmp_w1_k, cmp_w2_k, cmp_pos_v, cmp_w1_v, cmp_w2_v, norm_ffn_g, peer_w_q, peer_keys1, peer_keys2, peer_u, peer_v, norm_final_g):
    B, S, D = x.shape
    depth = w_in.shape[0]
    y = None
    for l in range(depth):
        if y is not None:
            x = (x.reshape(B * S, D) + y).reshape(B, S, D)
        x1, y = _layer(x, norm_mix_g[l], w_in[l], w_out[l], gm_ln_g[l], gm_ln_b[l], gm_spatial_w[l], gm_spatial_b[l],
                       cmp_pos_k[l], cmp_w1_k[l], cmp_w2_k[l], cmp_pos_v[l], cmp_w1_v[l], cmp_w2_v[l],
                       norm_ffn_g[l], peer_w_q[l], peer_keys1[l], peer_keys2[l], peer_u[l], peer_v[l])
        x = x1.reshape(B, S, D)
    return final_norm(x.reshape(B * S, D), y, norm_final_g).reshape(B, S, D)
```

```python
import functools
import math

import numpy as np
import jax
import jax.numpy as jnp
from jax import lax
from jax.experimental import pallas as pl
from jax.experimental.pallas import tpu as pltpu

D_MODEL = 2048
GM_GROUPS = 16
GM_GROUP_DIM = 128
GM_CHUNK = 128

NSA_HEADS = 16
NSA_KV_HEADS = 4
NSA_GROUP = 4
HEAD_DIM = 128
CMP_BLOCK = 32
CMP_STRIDE = 16
CMP_HIDDEN = 256
SLC_BLOCK = 64
N_SELECT = 16
WINDOW = 512
ROPE_THETA = 10000.0

PEER_HEADS = 8
PEER_N_KEYS = 128
PEER_TOPK = 16
PEER_SLOTS = PEER_HEADS * PEER_TOPK

EPS = 1e-6
NEG_BIG = -1e30
POS_BIG = 1e30

VMEM_LIMIT_BYTES = 56 * 1024 * 1024
LANE = 128
SUBLANE = 8

COL_U = 0
COL_V = 2048
COL_Q = 4096
COL_MERGE_A = 6144
COL_MERGE_B = 8192
COL_KV = 10240
COL_GATE = 13312
GATE_PAD = 128
COL_TOTAL = COL_GATE + NSA_KV_HEADS * GATE_PAD

BF16 = jnp.bfloat16
F32 = jnp.float32


def _params(*sem):
    return pltpu.CompilerParams(dimension_semantics=sem, vmem_limit_bytes=VMEM_LIMIT_BYTES)


def _gelu(x):
    return 0.5 * x * (1.0 + jnp.tanh(math.sqrt(2.0 / math.pi) * (x + 0.044715 * (x * x * x))))


def _sigmoid(x):
    return 1.0 / (1.0 + jnp.exp(-x))


def _rms(x, g):
    return x * lax.rsqrt(jnp.mean(x * x, axis=-1, keepdims=True) + EPS) * g


def _dot_nt(a, b, **kw):
    return lax.dot_general(a, b, (((1,), (1,)), ((), ())), preferred_element_type=F32, **kw)


def _rmsnorm_kernel(x_ref, g_ref, o_ref):
    o_ref[...] = _rms(x_ref[...], g_ref[...]).astype(o_ref.dtype)


def rmsnorm_cast(x, g, tm=512):
    T, D = x.shape
    return pl.pallas_call(
        _rmsnorm_kernel,
        grid=(T // tm,),
        in_specs=[pl.BlockSpec((tm, D), lambda i: (i, 0)), pl.BlockSpec((1, D), lambda i: (0, 0))],
        out_specs=pl.BlockSpec((tm, D), lambda i: (i, 0)),
        out_shape=jax.ShapeDtypeStruct((T, D), BF16),
        compiler_params=_params("parallel"),
        name="rmsnorm_cast",
    )(x, g.reshape(1, D))


def _mm_kernel(a_ref, w_ref, o_ref):
    o_ref[...] = jnp.dot(a_ref[...], w_ref[...], preferred_element_type=F32).astype(o_ref.dtype)


def matmul(a, w, tm, tn, out_dtype, name):
    M, K = a.shape
    N = w.shape[1]
    return pl.pallas_call(
        _mm_kernel,
        grid=(N // tn, M // tm),
        in_specs=[pl.BlockSpec((tm, K), lambda j, i: (i, 0)), pl.BlockSpec((K, tn), lambda j, i: (0, j))],
        out_specs=pl.BlockSpec((tm, tn), lambda j, i: (i, j)),
        out_shape=jax.ShapeDtypeStruct((M, N), out_dtype),
        compiler_params=_params("parallel", "parallel"),
        name=name,
    )(a, w)


def _gmlp_kernel(u_ref, v_ref, g_ref, b_ref, w_ref, bs_ref, o_ref):
    gv = _gelu(v_ref[...])
    mu = jnp.mean(gv, axis=-1, keepdims=True)
    xc = gv - mu
    vn = xc * lax.rsqrt(jnp.mean(xc * xc, axis=-1, keepdims=True) + EPS) * g_ref[...] + b_ref[...]
    vn = vn.astype(BF16)
    row = lax.broadcasted_iota(jnp.int32, (GM_CHUNK, GM_CHUNK), 0)
    col = lax.broadcasted_iota(jnp.int32, (GM_CHUNK, GM_CHUNK), 1)
    causal = row >= col
    bs = bs_ref[...]
    for g in range(GM_GROUPS):
        sl = slice(g * GM_GROUP_DIM, (g + 1) * GM_GROUP_DIM)
        w = jnp.where(causal, w_ref[g], 0.0).astype(BF16)
        z = jnp.dot(w, vn[:, sl], preferred_element_type=F32) + bs[:, g:g + 1]
        o_ref[:, sl] = (_gelu(u_ref[:, sl]) * z).astype(o_ref.dtype)


def gmlp_mixer(proj, ln_g, ln_b, w_s, b_s):
    T = proj.shape[0]
    W = GM_GROUPS * GM_GROUP_DIM
    return pl.pallas_call(
        _gmlp_kernel,
        grid=(T // GM_CHUNK,),
        in_specs=[
            pl.BlockSpec((GM_CHUNK, W), lambda i: (i, COL_U // W)),
            pl.BlockSpec((GM_CHUNK, W), lambda i: (i, COL_V // W)),
            pl.BlockSpec((1, W), lambda i: (0, 0)),
            pl.BlockSpec((1, W), lambda i: (0, 0)),
            pl.BlockSpec((GM_GROUPS, GM_CHUNK, GM_CHUNK), lambda i: (0, 0, 0)),
            pl.BlockSpec((GM_CHUNK, GM_GROUPS), lambda i: (0, 0)),
        ],
        out_specs=pl.BlockSpec((GM_CHUNK, W), lambda i: (i, 0)),
        out_shape=jax.ShapeDtypeStruct((T, W), BF16),
        compiler_params=_params("parallel"),
        name="gmlp_mixer",
    )(proj, proj, ln_g.reshape(1, W), ln_b.reshape(1, W), w_s, b_s.T)


def _rope(x, cos, sin_signed):
    return x * cos + pltpu.roll(x, HEAD_DIM // 2, axis=1) * sin_signed


def _compress(k_ref, pos_ref, w1_ref, w2_ref, nc):
    half = CMP_BLOCK // 2
    a = jnp.zeros((nc, CMP_HIDDEN), F32)
    b = jnp.zeros((nc, CMP_HIDDEN), F32)
    for l in range(half):
        kl = k_ref[0, pl.ds(l, nc, stride=CMP_STRIDE), :]
        a = a + jnp.dot((kl + pos_ref[l:l + 1, :]).astype(BF16), w1_ref[l], preferred_element_type=F32)
        b = b + jnp.dot((kl + pos_ref[half + l:half + l + 1, :]).astype(BF16), w1_ref[half + l],
                        preferred_element_type=F32)
    hid = a + pltpu.roll(b, nc - 1, axis=0)
    return jnp.dot(_gelu(hid).astype(BF16), w2_ref[...], preferred_element_type=F32)


def _nsa_prep_kernel(kc_ref, vc_ref, ks_ref, vs_ref, kw_ref, vw_ref, cos_ref, sin_ref,
                     pk_ref, w1k_ref, w2k_ref, pv_ref, w1v_ref, w2v_ref,
                     okc_ref, ovc_ref, oks_ref, ovs_ref, okw_ref, ovw_ref, *, nc):
    okc_ref[0, 0] = _compress(kc_ref, pk_ref, w1k_ref, w2k_ref, nc).astype(okc_ref.dtype)
    ovc_ref[0, 0] = _compress(vc_ref, pv_ref, w1v_ref, w2v_ref, nc).astype(ovc_ref.dtype)
    cos = cos_ref[...]
    sin = sin_ref[...]
    oks_ref[0, 0] = _rope(ks_ref[0], cos, sin).astype(oks_ref.dtype)
    okw_ref[0, 0] = _rope(kw_ref[0], cos, sin).astype(okw_ref.dtype)
    ovs_ref[0, 0] = vs_ref[0].astype(ovs_ref.dtype)
    ovw_ref[0, 0] = vw_ref[0].astype(ovw_ref.dtype)


def nsa_prep(proj3, cos, sin, pos_k, w1_k, w2_k, pos_v, w1_v, w2_v):
    B, S, _ = proj3.shape
    nc = S // CMP_STRIDE
    kvb = COL_KV // HEAD_DIM

    def kv_spec(i):
        return pl.BlockSpec((1, S, HEAD_DIM), lambda b, h, i=i: (b, 0, kvb + i * NSA_KV_HEADS + h))

    full2 = lambda shp: pl.BlockSpec(shp, lambda b, h: (0, 0))
    full3 = lambda shp: pl.BlockSpec(shp, lambda b, h: (0, 0, 0))
    out_c = pl.BlockSpec((1, 1, nc, HEAD_DIM), lambda b, h: (b, h, 0, 0))
    out_s = pl.BlockSpec((1, 1, S, HEAD_DIM), lambda b, h: (b, h, 0, 0))
    shp_c = jax.ShapeDtypeStruct((B, NSA_KV_HEADS, nc, HEAD_DIM), BF16)
    shp_s = jax.ShapeDtypeStruct((B, NSA_KV_HEADS, S, HEAD_DIM), BF16)
    w1k = w1_k.astype(BF16).reshape(CMP_BLOCK, HEAD_DIM, CMP_HIDDEN)
    w1v = w1_v.astype(BF16).reshape(CMP_BLOCK, HEAD_DIM, CMP_HIDDEN)
    return pl.pallas_call(
        functools.partial(_nsa_prep_kernel, nc=nc),
        grid=(B, NSA_KV_HEADS),
        in_specs=[kv_spec(i) for i in range(6)] + [
            full2((S, HEAD_DIM)), full2((S, HEAD_DIM)),
            full2((CMP_BLOCK, HEAD_DIM)), full3((CMP_BLOCK, HEAD_DIM, CMP_HIDDEN)), full2((CMP_HIDDEN, HEAD_DIM)),
            full2((CMP_BLOCK, HEAD_DIM)), full3((CMP_BLOCK, HEAD_DIM, CMP_HIDDEN)), full2((CMP_HIDDEN, HEAD_DIM)),
        ],
        out_specs=[out_c, out_c, out_s, out_s, out_s, out_s],
        out_shape=[shp_c, shp_c, shp_s, shp_s, shp_s, shp_s],
        compiler_params=_params("parallel", "parallel"),
        name="nsa_prep",
    )(proj3, proj3, proj3, proj3, proj3, proj3, cos, sin,
      pos_k, w1k, w2_k.astype(BF16), pos_v, w1v, w2_v.astype(BF16))


def _softmax_rows(s, mask):
    s = jnp.where(mask, s, NEG_BIG)
    m = jnp.max(s, axis=-1, keepdims=True)
    e = jnp.exp(s - m)
    return e / jnp.sum(e, axis=-1, keepdims=True)


def _nsa_attn_kernel(q_ref, gate_ref, kc_ref, vc_ref, ks_ref, vs_ref, kw_ref, vw_ref,
                     cos_ref, sin_ref, agg_ref, exp_ref, o_ref, m_ref, l_ref, acc_ref, *, tq, S, n_sel, wk):
    qt = pl.program_id(2)
    nc = S // CMP_STRIDE
    n_slc = S // SLC_BLOCK
    scale = HEAD_DIM ** -0.5
    pos = qt * tq + lax.broadcasted_iota(jnp.int32, (tq, 1), 0)
    cos = cos_ref[...]
    sin = sin_ref[...]
    gates = _sigmoid(gate_ref[0])

    qs = [q_ref[0, :, g * HEAD_DIM:(g + 1) * HEAD_DIM] for g in range(NSA_GROUP)]
    qr = [_rope(q, cos, sin).astype(BF16) for q in qs]

    kc = kc_ref[0, 0]
    vc = vc_ref[0, 0]
    cmp_end = lax.broadcasted_iota(jnp.int32, (1, nc), 1) * CMP_STRIDE + (CMP_BLOCK - 1)
    cmask = (cmp_end <= pos) & (cmp_end < S)
    any_c = pos >= CMP_BLOCK - 1
    o_cmp = []
    imp = jnp.zeros((tq, nc), F32)
    for g in range(NSA_GROUP):
        s = _dot_nt(qs[g].astype(BF16), kc) * scale
        p = jnp.where(any_c, _softmax_rows(s, cmask), 0.0)
        o_cmp.append(jnp.dot(p.astype(BF16), vc, preferred_element_type=F32))
        imp = imp + p

    imp_slc = jnp.dot(imp, agg_ref[...], preferred_element_type=F32, precision=lax.Precision.HIGHEST)
    blk = lax.broadcasted_iota(jnp.int32, (1, n_slc), 1)
    allowed = blk * SLC_BLOCK <= pos
    forced = (blk == 0) | (blk == pos // SLC_BLOCK)
    score = jnp.where(forced, POS_BIG, jnp.where(allowed, imp_slc, NEG_BIG))
    rank = jnp.zeros((tq, n_slc), F32)
    for i in range(n_slc):
        si = score[:, i:i + 1]
        earlier = (blk > i).astype(F32)
        rank = rank + jnp.where(si > score, 1.0, jnp.where(si == score, earlier, 0.0))
    sel = jnp.where(rank < n_sel, 1.0, 0.0).astype(BF16)

    m_ref[...] = jnp.full(m_ref.shape, NEG_BIG, F32)
    l_ref[...] = jnp.zeros(l_ref.shape, F32)
    acc_ref[...] = jnp.zeros(acc_ref.shape, F32)

    def slc_chunk(c, carry):
        off = pl.multiple_of(c * tq, tq)
        k_c = ks_ref[0, 0, pl.ds(off, tq), :]
        v_c = vs_ref[0, 0, pl.ds(off, tq), :]
        in_sel = jnp.dot(sel, exp_ref[c], preferred_element_type=F32) > 0.5
        valid = in_sel & (off + lax.broadcasted_iota(jnp.int32, (1, tq), 1) <= pos)
        for g in range(NSA_GROUP):
            s = jnp.where(valid, _dot_nt(qr[g], k_c) * scale, NEG_BIG)
            m_old = m_ref[g]
            m_new = jnp.maximum(m_old, jnp.max(s, axis=-1, keepdims=True))
            alpha = jnp.exp(m_old - m_new)
            p = jnp.exp(s - jnp.concatenate([m_new] * (tq // LANE), axis=1))
            l_ref[g] = alpha * l_ref[g] + jnp.sum(p, axis=-1, keepdims=True)
            acc_ref[g] = alpha * acc_ref[g] + jnp.dot(p.astype(BF16), v_c, preferred_element_type=F32)
            m_ref[g] = m_new
        return carry

    lax.fori_loop(0, qt + 1, slc_chunk, 0)
    o_slc = [acc_ref[g] / l_ref[g] for g in range(NSA_GROUP)]

    start = pl.multiple_of(jnp.maximum(qt * tq - (wk - tq), 0), tq)
    kw = kw_ref[0, 0, pl.ds(start, wk), :]
    vw = vw_ref[0, 0, pl.ds(start, wk), :]
    diff = pos - (start + lax.broadcasted_iota(jnp.int32, (1, wk), 1))
    wmask = (diff >= 0) & (diff < WINDOW)
    for g in range(NSA_GROUP):
        s = _dot_nt(qr[g], kw) * scale
        p = _softmax_rows(s, wmask)
        o_win = jnp.dot(p.astype(BF16), vw, preferred_element_type=F32)
        c = g * 3
        o = gates[:, c:c + 1] * o_cmp[g] + gates[:, c + 1:c + 2] * o_slc[g] + gates[:, c + 2:c + 3] * o_win
        o_ref[0, :, g * HEAD_DIM:(g + 1) * HEAD_DIM] = o.astype(o_ref.dtype)


def _selection_constants(S, tq):
    nc = S // CMP_STRIDE
    n_cmp = (S - CMP_BLOCK) // CMP_STRIDE + 1
    n_slc = S // SLC_BLOCK
    ratio = SLC_BLOCK // CMP_STRIDE
    span = CMP_BLOCK // CMP_STRIDE
    agg_w = np.convolve(np.ones(ratio), np.ones(span))
    agg = np.zeros((nc, n_slc), np.float32)
    for j in range(n_slc):
        for w in range(ratio + span - 1):
            c = ratio * j + w - (span - 1)
            if 0 <= c < n_cmp:
                agg[c, j] += agg_w[w]
    expand = (np.arange(S)[None, :] // SLC_BLOCK == np.arange(n_slc)[:, None]).astype(np.float32)
    expand = expand.reshape(n_slc, S // tq, tq).transpose(1, 0, 2)
    return jnp.asarray(agg), jnp.asarray(expand, dtype=BF16)


def nsa_attention(proj3, gate3, kc, vc, ks, vs, kw, vw, cos, sin, tq=256):
    B, S, _ = proj3.shape
    nc = S // CMP_STRIDE
    n_slc = S // SLC_BLOCK
    n_sel = min(N_SELECT, n_slc)
    wk = min(S, WINDOW + tq)
    agg, expand = _selection_constants(S, tq)
    qw = NSA_GROUP * HEAD_DIM
    cspec = pl.BlockSpec((1, 1, nc, HEAD_DIM), lambda b, h, t: (b, h, 0, 0))
    sspec = pl.BlockSpec((1, 1, S, HEAD_DIM), lambda b, h, t: (b, h, 0, 0))
    return pl.pallas_call(
        functools.partial(_nsa_attn_kernel, tq=tq, S=S, n_sel=n_sel, wk=wk),
        grid=(B, NSA_KV_HEADS, S // tq),
        in_specs=[
            pl.BlockSpec((1, tq, qw), lambda b, h, t: (b, t, COL_Q // qw + h)),
            pl.BlockSpec((1, tq, GATE_PAD), lambda b, h, t: (b, t, COL_GATE // GATE_PAD + h)),
            cspec, cspec, sspec, sspec, sspec, sspec,
            pl.BlockSpec((tq, HEAD_DIM), lambda b, h, t: (t, 0)),
            pl.BlockSpec((tq, HEAD_DIM), lambda b, h, t: (t, 0)),
            pl.BlockSpec((nc, n_slc), lambda b, h, t: (0, 0)),
            pl.BlockSpec((S // tq, n_slc, tq), lambda b, h, t: (0, 0, 0)),
        ],
        out_specs=pl.BlockSpec((1, tq, qw), lambda b, h, t: (b, t, h)),
        out_shape=jax.ShapeDtypeStruct((B, S, NSA_HEADS * HEAD_DIM), BF16),
        scratch_shapes=[pltpu.VMEM((NSA_GROUP, tq, HEAD_DIM), F32) for _ in range(3)],
        compiler_params=_params("parallel", "parallel", "parallel"),
        name="nsa_attention",
    )(proj3, gate3, kc, vc, ks, vs, kw, vw, cos, sin, agg, expand)


def _merge_out_kernel(ma_ref, mb_ref, oa_ref, ob_ref, x_ref, w_ref, g_ref, x1_ref, h_ref):
    m = _sigmoid(ma_ref[...]) * oa_ref[...].astype(F32) + _sigmoid(mb_ref[...]) * ob_ref[...].astype(F32)
    x1 = x_ref[...] + jnp.dot(m.astype(BF16), w_ref[...], preferred_element_type=F32)
    x1_ref[...] = x1
    h_ref[...] = _rms(x1, g_ref[...])


def merge_out(proj, o_a, o_b, x, w_out, g, tm=256):
    T, D = x.shape
    row = lambda c: pl.BlockSpec((tm, D), lambda i, c=c: (i, c))
    return pl.pallas_call(
        _merge_out_kernel,
        grid=(T // tm,),
        in_specs=[row(COL_MERGE_A // D), row(COL_MERGE_B // D), row(0), row(0), row(0),
                  pl.BlockSpec((D, D), lambda i: (0, 0)), pl.BlockSpec((1, D), lambda i: (0, 0))],
        out_specs=[row(0), row(0)],
        out_shape=[jax.ShapeDtypeStruct((T, D), F32), jax.ShapeDtypeStruct((T, D), F32)],
        compiler_params=_params("parallel"),
        name="merge_out",
    )(proj, proj, o_a, o_b, x, w_out, g.reshape(1, D))


def _topk_rows(vals, order, payload, k):
    big = float(2 ** 30)
    out_v, out_p = [], []
    for _ in range(k):
        m = jnp.max(vals, axis=0, keepdims=True)
        first = jnp.min(jnp.where(vals == m, order, big), axis=0, keepdims=True)
        hit = order == first
        out_v.append(m)
        out_p.append(first if payload is order else jnp.max(jnp.where(hit, payload, -1.0), axis=0, keepdims=True))
        vals = jnp.where(hit, -jnp.inf, vals)
    return jnp.concatenate(out_v, axis=0), jnp.concatenate(out_p, axis=0)


def _product_key_cells():
    pieces = []
    a = 0
    while a < PEER_TOPK and PEER_TOPK // (a + 1) > 1:
        n_b = PEER_TOPK // (a + 1)
        pieces.append((a, 1, n_b, -(-n_b // SUBLANE) * SUBLANE))
        a += 1
    pieces.append((a, PEER_TOPK - a, 1, 1))
    return pieces


def _peer_route_kernel(q_ref, k1_ref, k2_ref, e_ref, g_ref):
    tt = q_ref.shape[0]
    key_row = lax.broadcasted_iota(jnp.int32, (PEER_N_KEYS, tt), 0).astype(F32)
    pieces = _product_key_cells()
    flat = []
    for a0, n_a, n_b, rows in pieces:
        if n_a == 1:
            flat.append(a0 * PEER_TOPK + lax.broadcasted_iota(jnp.int32, (rows, tt), 0))
        else:
            flat.append((a0 + lax.broadcasted_iota(jnp.int32, (n_a, tt), 0)) * PEER_TOPK)
    flat = jnp.concatenate(flat, axis=0).astype(F32)
    for h in range(PEER_HEADS):
        tops = []
        for c, k_ref in enumerate((k1_ref, k2_ref)):
            qh = q_ref[:, (2 * h + c) * PEER_N_KEYS:(2 * h + c + 1) * PEER_N_KEYS]
            s = _dot_nt(k_ref[...], qh, precision=lax.Precision.HIGHEST)
            tops.append(_topk_rows(s, key_row, key_row, PEER_TOPK))
        (v1, i1), (v2, i2) = tops
        cand, cand_id = [], []
        for a0, n_a, n_b, rows in pieces:
            if n_a == 1:
                vals = v1[a0:a0 + 1, :] + v2[:rows, :]
                if n_b < rows:
                    vals = jnp.where(lax.broadcasted_iota(jnp.int32, (rows, tt), 0) < n_b, vals, -jnp.inf)
                cand.append(vals)
                cand_id.append(i1[a0:a0 + 1, :] * PEER_N_KEYS + i2[:rows, :])
            else:
                cand.append(v1[a0:a0 + n_a, :] + v2[0:1, :])
                cand_id.append(i1[a0:a0 + n_a, :] * PEER_N_KEYS + i2[0:1, :])
        top_s, top_e = _topk_rows(jnp.concatenate(cand, axis=0), flat, jnp.concatenate(cand_id, axis=0), PEER_TOPK)
        ex = jnp.exp(top_s - jnp.max(top_s, axis=0, keepdims=True))
        e_ref[h * PEER_TOPK:(h + 1) * PEER_TOPK, :] = top_e.astype(jnp.int32)
        g_ref[h * PEER_TOPK:(h + 1) * PEER_TOPK, :] = ex / jnp.sum(ex, axis=0, keepdims=True)


def peer_route(qp, keys1, keys2, tt=128):
    T, Q = qp.shape
    kspec = pl.BlockSpec(keys1.shape, lambda i: (0, 0))
    ospec = pl.BlockSpec((PEER_SLOTS, tt), lambda i: (0, i))
    return pl.pallas_call(
        _peer_route_kernel,
        grid=(T // tt,),
        in_specs=[pl.BlockSpec((tt, Q), lambda i: (i, 0)), kspec, kspec],
        out_specs=[ospec, ospec],
        out_shape=[jax.ShapeDtypeStruct((PEER_SLOTS, T), jnp.int32), jax.ShapeDtypeStruct((PEER_SLOTS, T), F32)],
        compiler_params=_params("parallel"),
        name="peer_route",
    )(qp, keys1, keys2)


X_ROWS = D_MODEL // LANE
UV_ROWS = 2 * X_ROWS
PEER_NSLOT = 8
DMA_PRIORITIES = 2


def _pack_uv(u, v):
    n_exp = u.shape[0]
    return jnp.concatenate([u.reshape(n_exp, X_ROWS, LANE), v.reshape(n_exp, X_ROWS, LANE)], axis=1).astype(BF16)


def _peer_apply_kernel(idx_ref, idx_next_ref, h_ref, gt_ref, uv_hbm, o_ref, *scratch, tb):
    bufs, p_ref, w_ref, sem = scratch[:PEER_NSLOT], scratch[PEER_NSLOT], scratch[PEER_NSLOT + 1], scratch[PEER_NSLOT + 2]
    ahead = PEER_NSLOT - 1
    step = pl.program_id(0)

    def row_copy(ids, t, j, slot, row):
        return pltpu.make_async_copy(uv_hbm.at[ids[t, j]], bufs[slot].at[pl.ds(row, UV_ROWS), :], sem.at[slot])

    def issue(ids, t, slot):
        for j in range(PEER_SLOTS):
            row_copy(ids, t, j, slot, j * UV_ROWS).start(priority=j % DMA_PRIORITIES)

    def issue_rolled(ids, t, slot):
        def body(j, carry):
            row_copy(ids, t, j, slot, pl.multiple_of(j * UV_ROWS, UV_ROWS)).start()
            return carry
        lax.fori_loop(0, PEER_SLOTS, body, 0)

    def wait_all(slot):
        pltpu.make_async_copy(bufs[(slot + 1) % PEER_NSLOT], bufs[slot], sem.at[slot]).wait()

    lane = lax.broadcasted_iota(jnp.int32, (PEER_SLOTS, tb), 1)

    def compute(t, slot):
        buf = bufs[slot]
        x = h_ref[t]
        for j in range(PEER_SLOTS):
            prod = buf[j * UV_ROWS:j * UV_ROWS + X_ROWS, :].astype(F32) * x
            p_ref[j * SUBLANE:(j + 1) * SUBLANE, :] = prod[:SUBLANE] + prod[SUBLANE:]
        acc = p_ref[pl.ds(0, PEER_SLOTS, stride=SUBLANE), :]
        for q in range(1, SUBLANE):
            acc = acc + p_ref[pl.ds(q, PEER_SLOTS, stride=SUBLANE), :]
        act = _gelu(jnp.sum(acc, axis=1, keepdims=True))
        gate = jnp.sum(jnp.where(lane == t, gt_ref[...], 0.0), axis=1, keepdims=True)
        w_ref[...] = jnp.broadcast_to(gate * act, (PEER_SLOTS, LANE))
        n_acc = 4
        out = [jnp.zeros((X_ROWS, LANE), F32) for _ in range(n_acc)]
        for j in range(PEER_SLOTS):
            base = j * UV_ROWS + X_ROWS
            out[j % n_acc] = out[j % n_acc] + buf[base:base + X_ROWS, :].astype(F32) * w_ref[j:j + 1, :]
        o_ref[t] = (out[0] + out[1]) + (out[2] + out[3])

    def group(g, carry):
        t0 = g * PEER_NSLOT
        for i in range(PEER_NSLOT):
            wait_all(i)
            issue(idx_ref, t0 + i + ahead, (i + ahead) % PEER_NSLOT)
            compute(t0 + i, i)
        return carry

    @pl.when(step == 0)
    def _():
        for t in range(ahead):
            issue_rolled(idx_ref, t, t)

    n_groups = tb // PEER_NSLOT
    lax.fori_loop(0, n_groups - 1, group, 0)
    t0 = (n_groups - 1) * PEER_NSLOT
    for i in range(PEER_NSLOT):
        wait_all(i)
        if i + ahead < PEER_NSLOT:
            issue(idx_ref, t0 + i + ahead, (i + ahead) % PEER_NSLOT)
        else:
            @pl.when(step + 1 < pl.num_programs(0))
            def _(i=i):
                issue(idx_next_ref, i + ahead - PEER_NSLOT, (i + ahead) % PEER_NSLOT)
        compute(t0 + i, i)


def peer_apply(experts, h, gate_t, uv, tb=64):
    T, D = h.shape
    n_blocks = T // tb
    tok_spec = pl.BlockSpec((tb, X_ROWS, LANE), lambda i: (i, 0, 0))
    gate_blocks = gate_t.reshape(PEER_SLOTS, n_blocks, tb).transpose(1, 0, 2)
    y = pl.pallas_call(
        functools.partial(_peer_apply_kernel, tb=tb),
        grid=(n_blocks,),
        in_specs=[
            pl.BlockSpec((tb, PEER_SLOTS), lambda i: (i, 0), memory_space=pltpu.SMEM),
            pl.BlockSpec((tb, PEER_SLOTS), lambda i: (jnp.minimum(i + 1, n_blocks - 1), 0), memory_space=pltpu.SMEM),
            tok_spec,
            pl.BlockSpec((None, PEER_SLOTS, tb), lambda i: (i, 0, 0)),
            pl.BlockSpec(memory_space=pl.ANY),
        ],
        out_specs=tok_spec,
        out_shape=jax.ShapeDtypeStruct((T, X_ROWS, LANE), F32),
        scratch_shapes=[pltpu.VMEM((PEER_SLOTS * UV_ROWS, LANE), BF16) for _ in range(PEER_NSLOT)] + [
            pltpu.VMEM((PEER_SLOTS * SUBLANE, LANE), F32), pltpu.VMEM((PEER_SLOTS, LANE), F32),
            pltpu.SemaphoreType.DMA((PEER_NSLOT,))],
        compiler_params=_params("arbitrary"),
        name="peer_apply",
    )(experts, experts, h.reshape(T, X_ROWS, LANE), gate_blocks, uv)
    return y.reshape(T, D)


def _final_kernel(x_ref, y_ref, g_ref, o_ref):
    o_ref[...] = _rms(x_ref[...] + y_ref[...], g_ref[...])


def final_norm(x1, y, g, tm=512):
    T, D = x1.shape
    spec = pl.BlockSpec((tm, D), lambda i: (i, 0))
    return pl.pallas_call(
        _final_kernel,
        grid=(T // tm,),
        in_specs=[spec, spec, pl.BlockSpec((1, D), lambda i: (0, 0))],
        out_specs=spec,
        out_shape=jax.ShapeDtypeStruct((T, D), F32),
        compiler_params=_params("parallel"),
        name="final_norm",
    )(x1, y, g.reshape(1, D))


def _pack_w_in(w):
    u, v, q, kv, gate, merge = jnp.split(w, [2048, 4096, 6144, 9216, 9264], axis=1)
    gate = gate.reshape(D_MODEL, NSA_KV_HEADS, NSA_GROUP * 3)
    gate = jnp.pad(gate, ((0, 0), (0, 0), (0, GATE_PAD - NSA_GROUP * 3))).reshape(D_MODEL, NSA_KV_HEADS * GATE_PAD)
    return jnp.concatenate([u, v, q, merge, kv, gate], axis=1).astype(BF16)


def _rope_tables(S):
    half = HEAD_DIM // 2
    inv = ROPE_THETA ** (-jnp.arange(half, dtype=F32) / half)
    ang = jnp.arange(S, dtype=F32)[:, None] * inv[None, :]
    cos = jnp.cos(ang)
    sin = jnp.sin(ang)
    return jnp.concatenate([cos, cos], axis=1), jnp.concatenate([-sin, sin], axis=1)


def _layer(x, norm_mix_g, w_in, w_out, gm_ln_g, gm_ln_b, gm_spatial_w, gm_spatial_b,
           cmp_pos_k, cmp_w1_k, cmp_w2_k, cmp_pos_v, cmp_w1_v, cmp_w2_v,
           norm_ffn_g, peer_w_q, peer_keys1, peer_keys2, peer_u, peer_v):
    B, S, D = x.shape
    T = B * S
    xt = x.reshape(T, D)
    h = rmsnorm_cast(xt, norm_mix_g)
    proj = matmul(h, _pack_w_in(w_in), 1024, 1536, F32, "in_proj")
    o_a = gmlp_mixer(proj, gm_ln_g, gm_ln_b, gm_spatial_w, gm_spatial_b)
    proj3 = proj.reshape(B, S, COL_TOTAL)
    cos, sin = _rope_tables(S)
    kc, vc, ks, vs, kw, vw = nsa_prep(proj3, cos, sin, cmp_pos_k, cmp_w1_k, cmp_w2_k, cmp_pos_v, cmp_w1_v, cmp_w2_v)
    o_b = nsa_attention(proj3, proj3, kc, vc, ks, vs, kw, vw, cos, sin).reshape(T, D)
    x1, h2 = merge_out(proj, o_a, o_b, xt, w_out.astype(BF16), norm_ffn_g)
    qp = matmul(h2.astype(BF16), peer_w_q.astype(BF16), 1024, 1024, F32, "peer_query")
    experts_t, gate_t = peer_route(qp, peer_keys1, peer_keys2)
    y = peer_apply(experts_t.T, h2, gate_t, _pack_uv(peer_u, peer_v))
    return x1, y


def kernel(x, norm_mix_g, w_in, w_out, gm_ln_g, gm_ln_b, gm_spatial_w, gm_spatial_b, cmp_pos_k, cmp_w1_k, cmp_w2_k, cmp_pos_v, cmp_w1_v, cmp_w2_v, norm_ffn_g, peer_w_q, peer_keys1, peer_keys2, peer_u, peer_v, norm_final_g):
    B, S, D = x.shape
    depth = w_in.shape[0]
    y = None
    for l in range(depth):
        if y is not None:
            x = (x.reshape(B * S, D) + y).reshape(B, S, D)
        x1, y = _layer(x, norm_mix_g[l], w_in[l], w_out[l], gm_ln_g[l], gm_ln_b[l], gm_spatial_w[l], gm_spatial_b[l],
                       cmp_pos_k[l], cmp_w1_k[l], cmp_w2_k[l], cmp_pos_v[l], cmp_w1_v[l], cmp_w2_v[l],
                       norm_ffn_g[l], peer_w_q[l], peer_keys1[l], peer_keys2[l], peer_u[l], peer_v[l])
        x = x1.reshape(B, S, D)
    return final_norm(x.reshape(B * S, D), y, norm_final_g).reshape(B, S, D)
```

```python
import functools
import math

import numpy as np
import jax
import jax.numpy as jnp
from jax import lax
from jax.experimental import pallas as pl
from jax.experimental.pallas import tpu as pltpu

D_MODEL = 2048
GM_GROUPS = 16
GM_GROUP_DIM = 128
GM_CHUNK = 128

NSA_HEADS = 16
NSA_KV_HEADS = 4
NSA_GROUP = 4
HEAD_DIM = 128
CMP_BLOCK = 32
CMP_STRIDE = 16
CMP_HIDDEN = 256
SLC_BLOCK = 64
N_SELECT = 16
WINDOW = 512
ROPE_THETA = 10000.0

PEER_HEADS = 8
PEER_N_KEYS = 128
PEER_TOPK = 16
PEER_SLOTS = PEER_HEADS * PEER_TOPK

EPS = 1e-6
NEG_BIG = -1e30
POS_BIG = 1e30

VMEM_LIMIT_BYTES = 56 * 1024 * 1024
LANE = 128
SUBLANE = 8

COL_U = 0
COL_V = 2048
COL_Q = 4096
COL_MERGE_A = 6144
COL_MERGE_B = 8192
COL_KV = 10240
COL_GATE = 13312
GATE_PAD = 128
COL_TOTAL = COL_GATE + NSA_KV_HEADS * GATE_PAD

BF16 = jnp.bfloat16
F32 = jnp.float32


def _params(*sem):
    return pltpu.CompilerParams(dimension_semantics=sem, vmem_limit_bytes=VMEM_LIMIT_BYTES)


def _gelu(x):
    return 0.5 * x * (1.0 + jnp.tanh(math.sqrt(2.0 / math.pi) * (x + 0.044715 * (x * x * x))))


def _sigmoid(x):
    return 1.0 / (1.0 + jnp.exp(-x))


def _rms(x, g):
    return x * lax.rsqrt(jnp.mean(x * x, axis=-1, keepdims=True) + EPS) * g


def _dot_nt(a, b, **kw):
    return lax.dot_general(a, b, (((1,), (1,)), ((), ())), preferred_element_type=F32, **kw)


def _rmsnorm_kernel(x_ref, g_ref, o_ref):
    o_ref[...] = _rms(x_ref[...], g_ref[...]).astype(o_ref.dtype)


def rmsnorm_cast(x, g, tm=512):
    T, D = x.shape
    return pl.pallas_call(
        _rmsnorm_kernel,
        grid=(T // tm,),
        in_specs=[pl.BlockSpec((tm, D), lambda i: (i, 0)), pl.BlockSpec((1, D), lambda i: (0, 0))],
        out_specs=pl.BlockSpec((tm, D), lambda i: (i, 0)),
        out_shape=jax.ShapeDtypeStruct((T, D), BF16),
        compiler_params=_params("parallel"),
        name="rmsnorm_cast",
    )(x, g.reshape(1, D))


def _mm_kernel(a_ref, w_ref, o_ref):
    o_ref[...] = jnp.dot(a_ref[...], w_ref[...], preferred_element_type=F32).astype(o_ref.dtype)


def matmul(a, w, tm, tn, out_dtype, name):
    M, K = a.shape
    N = w.shape[1]
    return pl.pallas_call(
        _mm_kernel,
        grid=(N // tn, M // tm),
        in_specs=[pl.BlockSpec((tm, K), lambda j, i: (i, 0)), pl.BlockSpec((K, tn), lambda j, i: (0, j))],
        out_specs=pl.BlockSpec((tm, tn), lambda j, i: (i, j)),
        out_shape=jax.ShapeDtypeStruct((M, N), out_dtype),
        compiler_params=_params("parallel", "parallel"),
        name=name,
    )(a, w)


def _gmlp_kernel(u_ref, v_ref, g_ref, b_ref, w_ref, bs_ref, o_ref):
    gv = _gelu(v_ref[...])
    mu = jnp.mean(gv, axis=-1, keepdims=True)
    xc = gv - mu
    vn = xc * lax.rsqrt(jnp.mean(xc * xc, axis=-1, keepdims=True) + EPS) * g_ref[...] + b_ref[...]
    vn = vn.astype(BF16)
    row = lax.broadcasted_iota(jnp.int32, (GM_CHUNK, GM_CHUNK), 0)
    col = lax.broadcasted_iota(jnp.int32, (GM_CHUNK, GM_CHUNK), 1)
    causal = row >= col
    bs = bs_ref[...]
    for g in range(GM_GROUPS):
        sl = slice(g * GM_GROUP_DIM, (g + 1) * GM_GROUP_DIM)
        w = jnp.where(causal, w_ref[g], 0.0).astype(BF16)
        z = jnp.dot(w, vn[:, sl], preferred_element_type=F32) + bs[:, g:g + 1]
        o_ref[:, sl] = (_gelu(u_ref[:, sl]) * z).astype(o_ref.dtype)


def gmlp_mixer(proj, ln_g, ln_b, w_s, b_s):
    T = proj.shape[0]
    W = GM_GROUPS * GM_GROUP_DIM
    return pl.pallas_call(
        _gmlp_kernel,
        grid=(T // GM_CHUNK,),
        in_specs=[
            pl.BlockSpec((GM_CHUNK, W), lambda i: (i, COL_U // W)),
            pl.BlockSpec((GM_CHUNK, W), lambda i: (i, COL_V // W)),
            pl.BlockSpec((1, W), lambda i: (0, 0)),
            pl.BlockSpec((1, W), lambda i: (0, 0)),
            pl.BlockSpec((GM_GROUPS, GM_CHUNK, GM_CHUNK), lambda i: (0, 0, 0)),
            pl.BlockSpec((GM_CHUNK, GM_GROUPS), lambda i: (0, 0)),
        ],
        out_specs=pl.BlockSpec((GM_CHUNK, W), lambda i: (i, 0)),
        out_shape=jax.ShapeDtypeStruct((T, W), BF16),
        compiler_params=_params("parallel"),
        name="gmlp_mixer",
    )(proj, proj, ln_g.reshape(1, W), ln_b.reshape(1, W), w_s, b_s.T)


def _rope(x, cos, sin_signed):
    return x * cos + pltpu.roll(x, HEAD_DIM // 2, axis=1) * sin_signed


def _compress(k_ref, pos_ref, w1_ref, w2_ref, nc):
    half = CMP_BLOCK // 2
    a = jnp.zeros((nc, CMP_HIDDEN), F32)
    b = jnp.zeros((nc, CMP_HIDDEN), F32)
    for l in range(half):
        kl = k_ref[0, pl.ds(l, nc, stride=CMP_STRIDE), :]
        a = a + jnp.dot((kl + pos_ref[l:l + 1, :]).astype(BF16), w1_ref[l], preferred_element_type=F32)
        b = b + jnp.dot((kl + pos_ref[half + l:half + l + 1, :]).astype(BF16), w1_ref[half + l],
                        preferred_element_type=F32)
    hid = a + pltpu.roll(b, nc - 1, axis=0)
    return jnp.dot(_gelu(hid).astype(BF16), w2_ref[...], preferred_element_type=F32)


def _nsa_prep_kernel(kc_ref, vc_ref, ks_ref, vs_ref, kw_ref, vw_ref, cos_ref, sin_ref,
                     pk_ref, w1k_ref, w2k_ref, pv_ref, w1v_ref, w2v_ref,
                     okc_ref, ovc_ref, oks_ref, ovs_ref, okw_ref, ovw_ref, *, nc):
    okc_ref[0, 0] = _compress(kc_ref, pk_ref, w1k_ref, w2k_ref, nc).astype(okc_ref.dtype)
    ovc_ref[0, 0] = _compress(vc_ref, pv_ref, w1v_ref, w2v_ref, nc).astype(ovc_ref.dtype)
    cos = cos_ref[...]
    sin = sin_ref[...]
    oks_ref[0, 0] = _rope(ks_ref[0], cos, sin).astype(oks_ref.dtype)
    okw_ref[0, 0] = _rope(kw_ref[0], cos, sin).astype(okw_ref.dtype)
    ovs_ref[0, 0] = vs_ref[0].astype(ovs_ref.dtype)
    ovw_ref[0, 0] = vw_ref[0].astype(ovw_ref.dtype)


def nsa_prep(proj3, cos, sin, pos_k, w1_k, w2_k, pos_v, w1_v, w2_v):
    B, S, _ = proj3.shape
    nc = S // CMP_STRIDE
    kvb = COL_KV // HEAD_DIM

    def kv_spec(i):
        return pl.BlockSpec((1, S, HEAD_DIM), lambda b, h, i=i: (b, 0, kvb + i * NSA_KV_HEADS + h))

    full2 = lambda shp: pl.BlockSpec(shp, lambda b, h: (0, 0))
    full3 = lambda shp: pl.BlockSpec(shp, lambda b, h: (0, 0, 0))
    out_c = pl.BlockSpec((1, 1, nc, HEAD_DIM), lambda b, h: (b, h, 0, 0))
    out_s = pl.BlockSpec((1, 1, S, HEAD_DIM), lambda b, h: (b, h, 0, 0))
    shp_c = jax.ShapeDtypeStruct((B, NSA_KV_HEADS, nc, HEAD_DIM), BF16)
    shp_s = jax.ShapeDtypeStruct((B, NSA_KV_HEADS, S, HEAD_DIM), BF16)
    w1k = w1_k.astype(BF16).reshape(CMP_BLOCK, HEAD_DIM, CMP_HIDDEN)
    w1v = w1_v.astype(BF16).reshape(CMP_BLOCK, HEAD_DIM, CMP_HIDDEN)
    return pl.pallas_call(
        functools.partial(_nsa_prep_kernel, nc=nc),
        grid=(B, NSA_KV_HEADS),
        in_specs=[kv_spec(i) for i in range(6)] + [
            full2((S, HEAD_DIM)), full2((S, HEAD_DIM)),
            full2((CMP_BLOCK, HEAD_DIM)), full3((CMP_BLOCK, HEAD_DIM, CMP_HIDDEN)), full2((CMP_HIDDEN, HEAD_DIM)),
            full2((CMP_BLOCK, HEAD_DIM)), full3((CMP_BLOCK, HEAD_DIM, CMP_HIDDEN)), full2((CMP_HIDDEN, HEAD_DIM)),
        ],
        out_specs=[out_c, out_c, out_s, out_s, out_s, out_s],
        out_shape=[shp_c, shp_c, shp_s, shp_s, shp_s, shp_s],
        compiler_params=_params("parallel", "parallel"),
        name="nsa_prep",
    )(proj3, proj3, proj3, proj3, proj3, proj3, cos, sin,
      pos_k, w1k, w2_k.astype(BF16), pos_v, w1v, w2_v.astype(BF16))


def _softmax_rows(s, mask):
    s = jnp.where(mask, s, NEG_BIG)
    m = jnp.max(s, axis=-1, keepdims=True)
    e = jnp.exp(s - m)
    return e / jnp.sum(e, axis=-1, keepdims=True)


def _nsa_attn_kernel(q_ref, gate_ref, kc_ref, vc_ref, ks_ref, vs_ref, kw_ref, vw_ref,
                     cos_ref, sin_ref, agg_ref, exp_ref, o_ref, m_ref, l_ref, acc_ref, *, tq, S, n_sel, wk):
    qt = pl.program_id(2)
    nc = S // CMP_STRIDE
    n_slc = S // SLC_BLOCK
    scale = HEAD_DIM ** -0.5
    pos = qt * tq + lax.broadcasted_iota(jnp.int32, (tq, 1), 0)
    cos = cos_ref[...]
    sin = sin_ref[...]
    gates = _sigmoid(gate_ref[0])

    qs = [q_ref[0, :, g * HEAD_DIM:(g + 1) * HEAD_DIM] for g in range(NSA_GROUP)]
    qr = [(_rope(q, cos, sin) * scale).astype(BF16) for q in qs]

    kc = kc_ref[0, 0]
    vc = vc_ref[0, 0]
    pos_t = qt * tq + lax.broadcasted_iota(jnp.int32, (1, tq), 1)
    cmp_end = lax.broadcasted_iota(jnp.int32, (nc, 1), 0) * CMP_STRIDE + (CMP_BLOCK - 1)
    cmask = (cmp_end <= pos_t) & (cmp_end < S)
    any_c = pos_t >= CMP_BLOCK - 1
    o_cmp = []
    imp = jnp.zeros((nc, tq), F32)
    for g in range(NSA_GROUP):
        s = jnp.where(cmask, _dot_nt(kc, (qs[g] * scale).astype(BF16)), NEG_BIG)
        e = jnp.exp(s - jnp.max(s, axis=0, keepdims=True))
        p = jnp.where(any_c, e / jnp.sum(e, axis=0, keepdims=True), 0.0)
        o_cmp.append(jnp.dot(p.T.astype(BF16), vc, preferred_element_type=F32))
        imp = imp + p

    imp_slc = jnp.dot(agg_ref[...], imp, preferred_element_type=F32, precision=lax.Precision.HIGHEST)
    blk = lax.broadcasted_iota(jnp.int32, (n_slc, tq), 0)
    allowed = blk * SLC_BLOCK <= pos_t
    forced = (blk == 0) | (blk == pos_t // SLC_BLOCK)
    score = jnp.where(forced, POS_BIG, jnp.where(allowed, imp_slc, NEG_BIG))
    rank = jnp.zeros((n_slc, tq), F32)
    for i in range(n_slc):
        si = score[i:i + 1, :]
        tie_before = jnp.where(blk > i, 1.0, 0.0)
        rank = rank + jnp.where(si > score, 1.0, jnp.where(si == score, tie_before, 0.0))
    sel_t = jnp.where(rank < n_sel, 1.0, 0.0)
    pad_rows = LANE - n_slc
    sel = jnp.concatenate([sel_t, jnp.zeros((pad_rows, tq), F32)], axis=0).T[:, :n_slc].astype(BF16)

    m_ref[...] = jnp.full(m_ref.shape, NEG_BIG, F32)
    l_ref[...] = jnp.zeros(l_ref.shape, F32)
    acc_ref[...] = jnp.zeros(acc_ref.shape, F32)

    def slc_chunk(c, carry):
        off = pl.multiple_of(c * tq, tq)
        k_c = ks_ref[0, 0, pl.ds(off, tq), :]
        v_c = vs_ref[0, 0, pl.ds(off, tq), :]
        in_sel = jnp.dot(sel, exp_ref[c], preferred_element_type=F32) > 0.5
        valid = in_sel & (off + lax.broadcasted_iota(jnp.int32, (1, tq), 1) <= pos)
        for g in range(NSA_GROUP):
            s = jnp.where(valid, _dot_nt(qr[g], k_c), NEG_BIG)
            m_old = m_ref[g]
            m_new = jnp.maximum(m_old, jnp.max(s, axis=-1, keepdims=True))
            alpha = jnp.exp(m_old - m_new)
            p = jnp.exp(s - jnp.concatenate([m_new] * (tq // LANE), axis=1))
            l_ref[g] = alpha * l_ref[g] + jnp.sum(p, axis=-1, keepdims=True)
            acc_ref[g] = alpha * acc_ref[g] + jnp.dot(p.astype(BF16), v_c, preferred_element_type=F32)
            m_ref[g] = m_new
        return carry

    lax.fori_loop(0, qt + 1, slc_chunk, 0)
    o_slc = [acc_ref[g] / l_ref[g] for g in range(NSA_GROUP)]

    start = pl.multiple_of(jnp.maximum(qt * tq - (wk - tq), 0), tq)
    kw = kw_ref[0, 0, pl.ds(start, wk), :]
    vw = vw_ref[0, 0, pl.ds(start, wk), :]
    diff = pos - (start + lax.broadcasted_iota(jnp.int32, (1, wk), 1))
    wmask = (diff >= 0) & (diff < WINDOW)
    for g in range(NSA_GROUP):
        s = _dot_nt(qr[g], kw)
        p = _softmax_rows(s, wmask)
        o_win = jnp.dot(p.astype(BF16), vw, preferred_element_type=F32)
        c = g * 3
        o = gates[:, c:c + 1] * o_cmp[g] + gates[:, c + 1:c + 2] * o_slc[g] + gates[:, c + 2:c + 3] * o_win
        o_ref[0, :, g * HEAD_DIM:(g + 1) * HEAD_DIM] = o.astype(o_ref.dtype)


def _selection_constants(S, tq):
    nc = S // CMP_STRIDE
    n_cmp = (S - CMP_BLOCK) // CMP_STRIDE + 1
    n_slc = S // SLC_BLOCK
    ratio = SLC_BLOCK // CMP_STRIDE
    span = CMP_BLOCK // CMP_STRIDE
    agg_w = np.convolve(np.ones(ratio), np.ones(span))
    agg = np.zeros((nc, n_slc), np.float32)
    for j in range(n_slc):
        for w in range(ratio + span - 1):
            c = ratio * j + w - (span - 1)
            if 0 <= c < n_cmp:
                agg[c, j] += agg_w[w]
    expand = (np.arange(S)[None, :] // SLC_BLOCK == np.arange(n_slc)[:, None]).astype(np.float32)
    expand = expand.reshape(n_slc, S // tq, tq).transpose(1, 0, 2)
    return jnp.asarray(agg.T), jnp.asarray(expand, dtype=BF16)


def nsa_attention(proj3, gate3, kc, vc, ks, vs, kw, vw, cos, sin, tq=256):
    B, S, _ = proj3.shape
    nc = S // CMP_STRIDE
    n_slc = S // SLC_BLOCK
    n_sel = min(N_SELECT, n_slc)
    wk = min(S, WINDOW + tq)
    agg, expand = _selection_constants(S, tq)
    qw = NSA_GROUP * HEAD_DIM
    cspec = pl.BlockSpec((1, 1, nc, HEAD_DIM), lambda b, h, t: (b, h, 0, 0))
    sspec = pl.BlockSpec((1, 1, S, HEAD_DIM), lambda b, h, t: (b, h, 0, 0))
    return pl.pallas_call(
        functools.partial(_nsa_attn_kernel, tq=tq, S=S, n_sel=n_sel, wk=wk),
        grid=(B, NSA_KV_HEADS, S // tq),
        in_specs=[
            pl.BlockSpec((1, tq, qw), lambda b, h, t: (b, t, COL_Q // qw + h)),
            pl.BlockSpec((1, tq, GATE_PAD), lambda b, h, t: (b, t, COL_GATE // GATE_PAD + h)),
            cspec, cspec, sspec, sspec, sspec, sspec,
            pl.BlockSpec((tq, HEAD_DIM), lambda b, h, t: (t, 0)),
            pl.BlockSpec((tq, HEAD_DIM), lambda b, h, t: (t, 0)),
            pl.BlockSpec((n_slc, nc), lambda b, h, t: (0, 0)),
            pl.BlockSpec((S // tq, n_slc, tq), lambda b, h, t: (0, 0, 0)),
        ],
        out_specs=pl.BlockSpec((1, tq, qw), lambda b, h, t: (b, t, h)),
        out_shape=jax.ShapeDtypeStruct((B, S, NSA_HEADS * HEAD_DIM), BF16),
        scratch_shapes=[pltpu.VMEM((NSA_GROUP, tq, HEAD_DIM), F32) for _ in range(3)],
        compiler_params=_params("parallel", "parallel", "parallel"),
        name="nsa_attention",
    )(proj3, gate3, kc, vc, ks, vs, kw, vw, cos, sin, agg, expand)


def _merge_out_kernel(ma_ref, mb_ref, oa_ref, ob_ref, x_ref, w_ref, g_ref, x1_ref, h_ref):
    m = _sigmoid(ma_ref[...]) * oa_ref[...].astype(F32) + _sigmoid(mb_ref[...]) * ob_ref[...].astype(F32)
    x1 = x_ref[...] + jnp.dot(m.astype(BF16), w_ref[...], preferred_element_type=F32)
    x1_ref[...] = x1
    h_ref[...] = _rms(x1, g_ref[...])


def merge_out(proj, o_a, o_b, x, w_out, g, tm=256):
    T, D = x.shape
    row = lambda c: pl.BlockSpec((tm, D), lambda i, c=c: (i, c))
    return pl.pallas_call(
        _merge_out_kernel,
        grid=(T // tm,),
        in_specs=[row(COL_MERGE_A // D), row(COL_MERGE_B // D), row(0), row(0), row(0),
                  pl.BlockSpec((D, D), lambda i: (0, 0)), pl.BlockSpec((1, D), lambda i: (0, 0))],
        out_specs=[row(0), row(0)],
        out_shape=[jax.ShapeDtypeStruct((T, D), F32), jax.ShapeDtypeStruct((T, D), F32)],
        compiler_params=_params("parallel"),
        name="merge_out",
    )(proj, proj, o_a, o_b, x, w_out, g.reshape(1, D))


def _topk_rows(vals, order, payload, k):
    big = float(2 ** 30)
    out_v, out_p = [], []
    for _ in range(k):
        m = jnp.max(vals, axis=0, keepdims=True)
        first = jnp.min(jnp.where(vals == m, order, big), axis=0, keepdims=True)
        hit = order == first
        out_v.append(m)
        out_p.append(first if payload is order else jnp.max(jnp.where(hit, payload, -1.0), axis=0, keepdims=True))
        vals = jnp.where(hit, -jnp.inf, vals)
    return jnp.concatenate(out_v, axis=0), jnp.concatenate(out_p, axis=0)


def _product_key_cells():
    pieces = []
    a = 0
    while a < PEER_TOPK and PEER_TOPK // (a + 1) > 1:
        n_b = PEER_TOPK // (a + 1)
        pieces.append((a, 1, n_b, -(-n_b // SUBLANE) * SUBLANE))
        a += 1
    pieces.append((a, PEER_TOPK - a, 1, 1))
    return pieces


def _peer_route_kernel(q_ref, k1_ref, k2_ref, e_ref, g_ref):
    tt = q_ref.shape[0]
    key_row = lax.broadcasted_iota(jnp.int32, (PEER_N_KEYS, tt), 0).astype(F32)
    pieces = _product_key_cells()
    flat = []
    for a0, n_a, n_b, rows in pieces:
        if n_a == 1:
            flat.append(a0 * PEER_TOPK + lax.broadcasted_iota(jnp.int32, (rows, tt), 0))
        else:
            flat.append((a0 + lax.broadcasted_iota(jnp.int32, (n_a, tt), 0)) * PEER_TOPK)
    flat = jnp.concatenate(flat, axis=0).astype(F32)
    for h in range(PEER_HEADS):
        tops = []
        for c, k_ref in enumerate((k1_ref, k2_ref)):
            qh = q_ref[:, (2 * h + c) * PEER_N_KEYS:(2 * h + c + 1) * PEER_N_KEYS]
            s = _dot_nt(k_ref[...], qh, precision=lax.Precision.HIGHEST)
            tops.append(_topk_rows(s, key_row, key_row, PEER_TOPK))
        (v1, i1), (v2, i2) = tops
        cand, cand_id = [], []
        for a0, n_a, n_b, rows in pieces:
            if n_a == 1:
                vals = v1[a0:a0 + 1, :] + v2[:rows, :]
                if n_b < rows:
                    vals = jnp.where(lax.broadcasted_iota(jnp.int32, (rows, tt), 0) < n_b, vals, -jnp.inf)
                cand.append(vals)
                cand_id.append(i1[a0:a0 + 1, :] * PEER_N_KEYS + i2[:rows, :])
            else:
                cand.append(v1[a0:a0 + n_a, :] + v2[0:1, :])
                cand_id.append(i1[a0:a0 + n_a, :] * PEER_N_KEYS + i2[0:1, :])
        top_s, top_e = _topk_rows(jnp.concatenate(cand, axis=0), flat, jnp.concatenate(cand_id, axis=0), PEER_TOPK)
        ex = jnp.exp(top_s - jnp.max(top_s, axis=0, keepdims=True))
        e_ref[h * PEER_TOPK:(h + 1) * PEER_TOPK, :] = top_e.astype(jnp.int32)
        g_ref[h * PEER_TOPK:(h + 1) * PEER_TOPK, :] = ex / jnp.sum(ex, axis=0, keepdims=True)


def peer_route(qp, keys1, keys2, tt=128):
    T, Q = qp.shape
    kspec = pl.BlockSpec(keys1.shape, lambda i: (0, 0))
    ospec = pl.BlockSpec((PEER_SLOTS, tt), lambda i: (0, i))
    return pl.pallas_call(
        _peer_route_kernel,
        grid=(T // tt,),
        in_specs=[pl.BlockSpec((tt, Q), lambda i: (i, 0)), kspec, kspec],
        out_specs=[ospec, ospec],
        out_shape=[jax.ShapeDtypeStruct((PEER_SLOTS, T), jnp.int32), jax.ShapeDtypeStruct((PEER_SLOTS, T), F32)],
        compiler_params=_params("parallel"),
        name="peer_route",
    )(qp, keys1, keys2)


X_ROWS = D_MODEL // LANE
UV_ROWS = 2 * X_ROWS
PEER_NSLOT = 16
DMA_PRIORITIES = 2


def _pack_uv(u, v):
    n_exp = u.shape[0]
    return jnp.concatenate([u.reshape(n_exp, X_ROWS, LANE), v.reshape(n_exp, X_ROWS, LANE)], axis=1).astype(BF16)


def _peer_apply_kernel(idx_ref, idx_next_ref, h_ref, gt_ref, uv_hbm, o_ref, *scratch, tb):
    bufs, p_ref, w_ref, sem = scratch[:PEER_NSLOT], scratch[PEER_NSLOT], scratch[PEER_NSLOT + 1], scratch[PEER_NSLOT + 2]
    ahead = PEER_NSLOT - 1
    step = pl.program_id(0)

    def row_copy(ids, t, j, slot, row):
        return pltpu.make_async_copy(uv_hbm.at[ids[t, j]], bufs[slot].at[pl.ds(row, UV_ROWS), :], sem.at[slot])

    def issue(ids, t, slot):
        for j in range(PEER_SLOTS):
            row_copy(ids, t, j, slot, j * UV_ROWS).start(priority=j % DMA_PRIORITIES)

    def issue_rolled(ids, t, slot):
        def body(j, carry):
            row_copy(ids, t, j, slot, pl.multiple_of(j * UV_ROWS, UV_ROWS)).start()
            return carry
        lax.fori_loop(0, PEER_SLOTS, body, 0)

    def wait_all(slot):
        pltpu.make_async_copy(bufs[(slot + 1) % PEER_NSLOT], bufs[slot], sem.at[slot]).wait()

    lane = lax.broadcasted_iota(jnp.int32, (PEER_SLOTS, tb), 1)

    def compute(t, slot):
        buf = bufs[slot]
        x = h_ref[t]
        for j in range(PEER_SLOTS):
            prod = buf[j * UV_ROWS:j * UV_ROWS + X_ROWS, :].astype(F32) * x
            p_ref[j * SUBLANE:(j + 1) * SUBLANE, :] = prod[:SUBLANE] + prod[SUBLANE:]
        acc = p_ref[pl.ds(0, PEER_SLOTS, stride=SUBLANE), :]
        for q in range(1, SUBLANE):
            acc = acc + p_ref[pl.ds(q, PEER_SLOTS, stride=SUBLANE), :]
        act = _gelu(jnp.sum(acc, axis=1, keepdims=True))
        gate = jnp.sum(jnp.where(lane == t, gt_ref[...], 0.0), axis=1, keepdims=True)
        w_ref[...] = jnp.broadcast_to(gate * act, (PEER_SLOTS, LANE))
        n_acc = 4
        out = [jnp.zeros((X_ROWS, LANE), F32) for _ in range(n_acc)]
        for j in range(PEER_SLOTS):
            base = j * UV_ROWS + X_ROWS
            out[j % n_acc] = out[j % n_acc] + buf[base:base + X_ROWS, :].astype(F32) * w_ref[j:j + 1, :]
        o_ref[t] = (out[0] + out[1]) + (out[2] + out[3])

    def group(g, carry):
        t0 = g * PEER_NSLOT
        for i in range(PEER_NSLOT):
            wait_all(i)
            issue(idx_ref, t0 + i + ahead, (i + ahead) % PEER_NSLOT)
            compute(t0 + i, i)
        return carry

    @pl.when(step == 0)
    def _():
        for t in range(ahead):
            issue_rolled(idx_ref, t, t)

    n_groups = tb // PEER_NSLOT
    lax.fori_loop(0, n_groups - 1, group, 0)
    t0 = (n_groups - 1) * PEER_NSLOT
    for i in range(PEER_NSLOT):
        wait_all(i)
        if i + ahead < PEER_NSLOT:
            issue(idx_ref, t0 + i + ahead, (i + ahead) % PEER_NSLOT)
        else:
            @pl.when(step + 1 < pl.num_programs(0))
            def _(i=i):
                issue(idx_next_ref, i + ahead - PEER_NSLOT, (i + ahead) % PEER_NSLOT)
        compute(t0 + i, i)


def peer_apply(experts, h, gate_t, uv, tb=64):
    T, D = h.shape
    n_blocks = T // tb
    tok_spec = pl.BlockSpec((tb, X_ROWS, LANE), lambda i: (i, 0, 0))
    gate_blocks = gate_t.reshape(PEER_SLOTS, n_blocks, tb).transpose(1, 0, 2)
    y = pl.pallas_call(
        functools.partial(_peer_apply_kernel, tb=tb),
        grid=(n_blocks,),
        in_specs=[
            pl.BlockSpec((tb, PEER_SLOTS), lambda i: (i, 0), memory_space=pltpu.SMEM),
            pl.BlockSpec((tb, PEER_SLOTS), lambda i: (jnp.minimum(i + 1, n_blocks - 1), 0), memory_space=pltpu.SMEM),
            tok_spec,
            pl.BlockSpec((None, PEER_SLOTS, tb), lambda i: (i, 0, 0)),
            pl.BlockSpec(memory_space=pl.ANY),
        ],
        out_specs=tok_spec,
        out_shape=jax.ShapeDtypeStruct((T, X_ROWS, LANE), F32),
        scratch_shapes=[pltpu.VMEM((PEER_SLOTS * UV_ROWS, LANE), BF16) for _ in range(PEER_NSLOT)] + [
            pltpu.VMEM((PEER_SLOTS * SUBLANE, LANE), F32), pltpu.VMEM((PEER_SLOTS, LANE), F32),
            pltpu.SemaphoreType.DMA((PEER_NSLOT,))],
        compiler_params=_params("arbitrary"),
        name="peer_apply",
    )(experts, experts, h.reshape(T, X_ROWS, LANE), gate_blocks, uv)
    return y.reshape(T, D)


def _final_kernel(x_ref, y_ref, g_ref, o_ref):
    o_ref[...] = _rms(x_ref[...] + y_ref[...], g_ref[...])


def final_norm(x1, y, g, tm=512):
    T, D = x1.shape
    spec = pl.BlockSpec((tm, D), lambda i: (i, 0))
    return pl.pallas_call(
        _final_kernel,
        grid=(T // tm,),
        in_specs=[spec, spec, pl.BlockSpec((1, D), lambda i: (0, 0))],
        out_specs=spec,
        out_shape=jax.ShapeDtypeStruct((T, D), F32),
        compiler_params=_params("parallel"),
        name="final_norm",
    )(x1, y, g.reshape(1, D))


def _pack_w_in(w):
    u, v, q, kv, gate, merge = jnp.split(w, [2048, 4096, 6144, 9216, 9264], axis=1)
    gate = gate.reshape(D_MODEL, NSA_KV_HEADS, NSA_GROUP * 3)
    gate = jnp.pad(gate, ((0, 0), (0, 0), (0, GATE_PAD - NSA_GROUP * 3))).reshape(D_MODEL, NSA_KV_HEADS * GATE_PAD)
    return jnp.concatenate([u, v, q, merge, kv, gate], axis=1).astype(BF16)


def _rope_tables(S):
    half = HEAD_DIM // 2
    inv = ROPE_THETA ** (-jnp.arange(half, dtype=F32) / half)
    ang = jnp.arange(S, dtype=F32)[:, None] * inv[None, :]
    cos = jnp.cos(ang)
    sin = jnp.sin(ang)
    return jnp.concatenate([cos, cos], axis=1), jnp.concatenate([-sin, sin], axis=1)


def _layer(x, norm_mix_g, w_in, w_out, gm_ln_g, gm_ln_b, gm_spatial_w, gm_spatial_b,
           cmp_pos_k, cmp_w1_k, cmp_w2_k, cmp_pos_v, cmp_w1_v, cmp_w2_v,
           norm_ffn_g, peer_w_q, peer_keys1, peer_keys2, peer_u, peer_v):
    B, S, D = x.shape
    T = B * S
    xt = x.reshape(T, D)
    h = rmsnorm_cast(xt, norm_mix_g)
    proj = matmul(h, _pack_w_in(w_in), 1024, 1536, F32, "in_proj")
    o_a = gmlp_mixer(proj, gm_ln_g, gm_ln_b, gm_spatial_w, gm_spatial_b)
    proj3 = proj.reshape(B, S, COL_TOTAL)
    cos, sin = _rope_tables(S)
    kc, vc, ks, vs, kw, vw = nsa_prep(proj3, cos, sin, cmp_pos_k, cmp_w1_k, cmp_w2_k, cmp_pos_v, cmp_w1_v, cmp_w2_v)
    o_b = nsa_attention(proj3, proj3, kc, vc, ks, vs, kw, vw, cos, sin).reshape(T, D)
    x1, h2 = merge_out(proj, o_a, o_b, xt, w_out.astype(BF16), norm_ffn_g)
    qp = matmul(h2.astype(BF16), peer_w_q.astype(BF16), 1024, 1024, F32, "peer_query")
    experts_t, gate_t = peer_route(qp, peer_keys1, peer_keys2)
    y = peer_apply(experts_t.T, h2, gate_t, _pack_uv(peer_u, peer_v))
    return x1, y


def kernel(x, norm_mix_g, w_in, w_out, gm_ln_g, gm_ln_b, gm_spatial_w, gm_spatial_b, cmp_pos_k, cmp_w1_k, cmp_w2_k, cmp_pos_v, cmp_w1_v, cmp_w2_v, norm_ffn_g, peer_w_q, peer_keys1, peer_keys2, peer_u, peer_v, norm_final_g):
    B, S, D = x.shape
    depth = w_in.shape[0]
    y = None
    for l in range(depth):
        if y is not None:
            x = (x.reshape(B * S, D) + y).reshape(B, S, D)
        x1, y = _layer(x, norm_mix_g[l], w_in[l], w_out[l], gm_ln_g[l], gm_ln_b[l], gm_spatial_w[l], gm_spatial_b[l],
                       cmp_pos_k[l], cmp_w1_k[l], cmp_w2_k[l], cmp_pos_v[l], cmp_w1_v[l], cmp_w2_v[l],
                       norm_ffn_g[l], peer_w_q[l], peer_keys1[l], peer_keys2[l], peer_u[l], peer_v[l])
        x = x1.reshape(B, S, D)
    return final_norm(x.reshape(B * S, D), y, norm_final_g).reshape(B, S, D)
```

```python
import functools
import math

import numpy as np
import jax
import jax.numpy as jnp
from jax import lax
from jax.experimental import pallas as pl
from jax.experimental.pallas import tpu as pltpu

D_MODEL = 2048
GM_GROUPS = 16
GM_GROUP_DIM = 128
GM_CHUNK = 128

NSA_HEADS = 16
NSA_KV_HEADS = 4
NSA_GROUP = 4
HEAD_DIM = 128
CMP_BLOCK = 32
CMP_STRIDE = 16
CMP_HIDDEN = 256
SLC_BLOCK = 64
N_SELECT = 16
WINDOW = 512
ROPE_THETA = 10000.0

PEER_HEADS = 8
PEER_N_KEYS = 128
PEER_TOPK = 16
PEER_SLOTS = PEER_HEADS * PEER_TOPK

EPS = 1e-6
NEG_BIG = -1e30
POS_BIG = 1e30

VMEM_LIMIT_BYTES = 56 * 1024 * 1024
LANE = 128
SUBLANE = 8

COL_U = 0
COL_V = 2048
COL_Q = 4096
COL_MERGE_A = 6144
COL_MERGE_B = 8192
COL_A_TOTAL = 10240
COL_KV = 0
COL_GATE = 3072
GATE_PAD = 128
COL_B_TOTAL = COL_GATE + NSA_KV_HEADS * GATE_PAD

BF16 = jnp.bfloat16
F32 = jnp.float32


def _params(*sem):
    return pltpu.CompilerParams(dimension_semantics=sem, vmem_limit_bytes=VMEM_LIMIT_BYTES)


def _gelu(x):
    return 0.5 * x * (1.0 + jnp.tanh(math.sqrt(2.0 / math.pi) * (x + 0.044715 * (x * x * x))))


def _sigmoid(x):
    return 1.0 / (1.0 + jnp.exp(-x))


def _rms(x, g):
    return x * lax.rsqrt(jnp.mean(x * x, axis=-1, keepdims=True) + EPS) * g


def _dot_nt(a, b, **kw):
    return lax.dot_general(a, b, (((1,), (1,)), ((), ())), preferred_element_type=F32, **kw)


def _rmsnorm_kernel(x_ref, g_ref, o_ref):
    o_ref[...] = _rms(x_ref[...], g_ref[...]).astype(o_ref.dtype)


def rmsnorm_cast(x, g, tm=512):
    T, D = x.shape
    return pl.pallas_call(
        _rmsnorm_kernel,
        grid=(T // tm,),
        in_specs=[pl.BlockSpec((tm, D), lambda i: (i, 0)), pl.BlockSpec((1, D), lambda i: (0, 0))],
        out_specs=pl.BlockSpec((tm, D), lambda i: (i, 0)),
        out_shape=jax.ShapeDtypeStruct((T, D), BF16),
        compiler_params=_params("parallel"),
        name="rmsnorm_cast",
    )(x, g.reshape(1, D))


def _mm_kernel(a_ref, w_ref, o_ref):
    o_ref[...] = jnp.dot(a_ref[...], w_ref[...], preferred_element_type=F32).astype(o_ref.dtype)


def matmul(a, w, tm, tn, out_dtype, name):
    M, K = a.shape
    N = w.shape[1]
    return pl.pallas_call(
        _mm_kernel,
        grid=(N // tn, M // tm),
        in_specs=[pl.BlockSpec((tm, K), lambda j, i: (i, 0)), pl.BlockSpec((K, tn), lambda j, i: (0, j))],
        out_specs=pl.BlockSpec((tm, tn), lambda j, i: (i, j)),
        out_shape=jax.ShapeDtypeStruct((M, N), out_dtype),
        compiler_params=_params("parallel", "parallel"),
        name=name,
    )(a, w)


def _gmlp_kernel(u_ref, v_ref, g_ref, b_ref, w_ref, bs_ref, o_ref):
    gv = _gelu(v_ref[...].astype(F32))
    mu = jnp.mean(gv, axis=-1, keepdims=True)
    xc = gv - mu
    vn = xc * lax.rsqrt(jnp.mean(xc * xc, axis=-1, keepdims=True) + EPS) * g_ref[...] + b_ref[...]
    vn = vn.astype(BF16)
    row = lax.broadcasted_iota(jnp.int32, (GM_CHUNK, GM_CHUNK), 0)
    col = lax.broadcasted_iota(jnp.int32, (GM_CHUNK, GM_CHUNK), 1)
    causal = row >= col
    bs = bs_ref[...]
    for g in range(GM_GROUPS):
        sl = slice(g * GM_GROUP_DIM, (g + 1) * GM_GROUP_DIM)
        w = jnp.where(causal, w_ref[g], 0.0).astype(BF16)
        z = jnp.dot(w, vn[:, sl], preferred_element_type=F32) + bs[:, g:g + 1]
        o_ref[:, sl] = (_gelu(u_ref[:, sl].astype(F32)) * z).astype(o_ref.dtype)


def gmlp_mixer(proj, ln_g, ln_b, w_s, b_s):
    T = proj.shape[0]
    W = GM_GROUPS * GM_GROUP_DIM
    return pl.pallas_call(
        _gmlp_kernel,
        grid=(T // GM_CHUNK,),
        in_specs=[
            pl.BlockSpec((GM_CHUNK, W), lambda i: (i, COL_U // W)),
            pl.BlockSpec((GM_CHUNK, W), lambda i: (i, COL_V // W)),
            pl.BlockSpec((1, W), lambda i: (0, 0)),
            pl.BlockSpec((1, W), lambda i: (0, 0)),
            pl.BlockSpec((GM_GROUPS, GM_CHUNK, GM_CHUNK), lambda i: (0, 0, 0)),
            pl.BlockSpec((GM_CHUNK, GM_GROUPS), lambda i: (0, 0)),
        ],
        out_specs=pl.BlockSpec((GM_CHUNK, W), lambda i: (i, 0)),
        out_shape=jax.ShapeDtypeStruct((T, W), BF16),
        compiler_params=_params("parallel"),
        name="gmlp_mixer",
    )(proj, proj, ln_g.reshape(1, W), ln_b.reshape(1, W), w_s, b_s.T)


def _rope(x, cos, sin_signed):
    return x * cos + pltpu.roll(x, HEAD_DIM // 2, axis=1) * sin_signed


def _compress(k_ref, pos_ref, w1_ref, w2_ref, nc):
    half = CMP_BLOCK // 2
    a = jnp.zeros((nc, CMP_HIDDEN), F32)
    b = jnp.zeros((nc, CMP_HIDDEN), F32)
    for l in range(half):
        kl = k_ref[0, pl.ds(l, nc, stride=CMP_STRIDE), :]
        a = a + jnp.dot((kl + pos_ref[l:l + 1, :]).astype(BF16), w1_ref[l], preferred_element_type=F32)
        b = b + jnp.dot((kl + pos_ref[half + l:half + l + 1, :]).astype(BF16), w1_ref[half + l],
                        preferred_element_type=F32)
    hid = a + pltpu.roll(b, nc - 1, axis=0)
    return jnp.dot(_gelu(hid).astype(BF16), w2_ref[...], preferred_element_type=F32)


def _nsa_prep_kernel(kc_ref, vc_ref, ks_ref, vs_ref, kw_ref, vw_ref, cos_ref, sin_ref,
                     pk_ref, w1k_ref, w2k_ref, pv_ref, w1v_ref, w2v_ref,
                     okc_ref, ovc_ref, oks_ref, ovs_ref, okw_ref, ovw_ref, *, nc):
    okc_ref[0, 0] = _compress(kc_ref, pk_ref, w1k_ref, w2k_ref, nc).astype(okc_ref.dtype)
    ovc_ref[0, 0] = _compress(vc_ref, pv_ref, w1v_ref, w2v_ref, nc).astype(ovc_ref.dtype)
    cos = cos_ref[...]
    sin = sin_ref[...]
    oks_ref[0, 0] = _rope(ks_ref[0], cos, sin).astype(oks_ref.dtype)
    okw_ref[0, 0] = _rope(kw_ref[0], cos, sin).astype(okw_ref.dtype)
    ovs_ref[0, 0] = vs_ref[0].astype(ovs_ref.dtype)
    ovw_ref[0, 0] = vw_ref[0].astype(ovw_ref.dtype)


def nsa_prep(proj3, cos, sin, pos_k, w1_k, w2_k, pos_v, w1_v, w2_v):
    B, S, _ = proj3.shape
    nc = S // CMP_STRIDE
    kvb = COL_KV // HEAD_DIM

    def kv_spec(i):
        return pl.BlockSpec((1, S, HEAD_DIM), lambda b, h, i=i: (b, 0, kvb + i * NSA_KV_HEADS + h))

    full2 = lambda shp: pl.BlockSpec(shp, lambda b, h: (0, 0))
    full3 = lambda shp: pl.BlockSpec(shp, lambda b, h: (0, 0, 0))
    out_c = pl.BlockSpec((1, 1, nc, HEAD_DIM), lambda b, h: (b, h, 0, 0))
    out_s = pl.BlockSpec((1, 1, S, HEAD_DIM), lambda b, h: (b, h, 0, 0))
    shp_c = jax.ShapeDtypeStruct((B, NSA_KV_HEADS, nc, HEAD_DIM), BF16)
    shp_s = jax.ShapeDtypeStruct((B, NSA_KV_HEADS, S, HEAD_DIM), BF16)
    w1k = w1_k.astype(BF16).reshape(CMP_BLOCK, HEAD_DIM, CMP_HIDDEN)
    w1v = w1_v.astype(BF16).reshape(CMP_BLOCK, HEAD_DIM, CMP_HIDDEN)
    return pl.pallas_call(
        functools.partial(_nsa_prep_kernel, nc=nc),
        grid=(B, NSA_KV_HEADS),
        in_specs=[kv_spec(i) for i in range(6)] + [
            full2((S, HEAD_DIM)), full2((S, HEAD_DIM)),
            full2((CMP_BLOCK, HEAD_DIM)), full3((CMP_BLOCK, HEAD_DIM, CMP_HIDDEN)), full2((CMP_HIDDEN, HEAD_DIM)),
            full2((CMP_BLOCK, HEAD_DIM)), full3((CMP_BLOCK, HEAD_DIM, CMP_HIDDEN)), full2((CMP_HIDDEN, HEAD_DIM)),
        ],
        out_specs=[out_c, out_c, out_s, out_s, out_s, out_s],
        out_shape=[shp_c, shp_c, shp_s, shp_s, shp_s, shp_s],
        compiler_params=_params("parallel", "parallel"),
        name="nsa_prep",
    )(proj3, proj3, proj3, proj3, proj3, proj3, cos, sin,
      pos_k, w1k, w2_k.astype(BF16), pos_v, w1v, w2_v.astype(BF16))


def _softmax_rows(s, mask):
    s = jnp.where(mask, s, NEG_BIG)
    m = jnp.max(s, axis=-1, keepdims=True)
    e = jnp.exp(s - m)
    return e / jnp.sum(e, axis=-1, keepdims=True)


def _nsa_attn_kernel(q_ref, gate_ref, kc_ref, vc_ref, ks_ref, vs_ref, kw_ref, vw_ref,
                     cos_ref, sin_ref, agg_ref, exp_ref, o_ref, m_ref, l_ref, acc_ref, *, tq, S, n_sel, wk):
    qt = pl.program_id(2)
    nc = S // CMP_STRIDE
    n_slc = S // SLC_BLOCK
    scale = HEAD_DIM ** -0.5
    pos = qt * tq + lax.broadcasted_iota(jnp.int32, (tq, 1), 0)
    cos = cos_ref[...]
    sin = sin_ref[...]
    gates = _sigmoid(gate_ref[0])

    qs = [q_ref[0, :, g * HEAD_DIM:(g + 1) * HEAD_DIM].astype(F32) for g in range(NSA_GROUP)]
    qr = [(_rope(q, cos, sin) * scale).astype(BF16) for q in qs]

    kc = kc_ref[0, 0]
    vc = vc_ref[0, 0]
    pos_t = qt * tq + lax.broadcasted_iota(jnp.int32, (1, tq), 1)
    cmp_end = lax.broadcasted_iota(jnp.int32, (nc, 1), 0) * CMP_STRIDE + (CMP_BLOCK - 1)
    cmask = (cmp_end <= pos_t) & (cmp_end < S)
    any_c = pos_t >= CMP_BLOCK - 1
    o_cmp = []
    imp = jnp.zeros((nc, tq), F32)
    for g in range(NSA_GROUP):
        s = jnp.where(cmask, _dot_nt(kc, (qs[g] * scale).astype(BF16)), NEG_BIG)
        e = jnp.exp(s - jnp.max(s, axis=0, keepdims=True))
        p = jnp.where(any_c, e / jnp.sum(e, axis=0, keepdims=True), 0.0)
        o_cmp.append(jnp.dot(p.T.astype(BF16), vc, preferred_element_type=F32))
        imp = imp + p

    imp_slc = jnp.dot(agg_ref[...], imp, preferred_element_type=F32, precision=lax.Precision.HIGHEST)
    blk = lax.broadcasted_iota(jnp.int32, (n_slc, tq), 0)
    allowed = blk * SLC_BLOCK <= pos_t
    forced = (blk == 0) | (blk == pos_t // SLC_BLOCK)
    score = jnp.where(forced, POS_BIG, jnp.where(allowed, imp_slc, NEG_BIG))
    rank = jnp.zeros((n_slc, tq), F32)
    for i in range(n_slc):
        si = score[i:i + 1, :]
        tie_before = jnp.where(blk > i, 1.0, 0.0)
        rank = rank + jnp.where(si > score, 1.0, jnp.where(si == score, tie_before, 0.0))
    sel_t = jnp.where(rank < n_sel, 1.0, 0.0)
    pad_rows = LANE - n_slc
    sel = jnp.concatenate([sel_t, jnp.zeros((pad_rows, tq), F32)], axis=0).T[:, :n_slc].astype(BF16)

    m_ref[...] = jnp.full(m_ref.shape, NEG_BIG, F32)
    l_ref[...] = jnp.zeros(l_ref.shape, F32)
    acc_ref[...] = jnp.zeros(acc_ref.shape, F32)

    def slc_chunk(c, carry):
        off = pl.multiple_of(c * tq, tq)
        k_c = ks_ref[0, 0, pl.ds(off, tq), :]
        v_c = vs_ref[0, 0, pl.ds(off, tq), :]
        in_sel = jnp.dot(sel, exp_ref[c], preferred_element_type=F32) > 0.5
        valid = in_sel & (off + lax.broadcasted_iota(jnp.int32, (1, tq), 1) <= pos)
        for g in range(NSA_GROUP):
            s = jnp.where(valid, _dot_nt(qr[g], k_c), NEG_BIG)
            m_old = m_ref[g]
            m_new = jnp.maximum(m_old, jnp.max(s, axis=-1, keepdims=True))
            alpha = jnp.exp(m_old - m_new)
            p = jnp.exp(s - jnp.concatenate([m_new] * (tq // LANE), axis=1))
            l_ref[g] = alpha * l_ref[g] + jnp.sum(p, axis=-1, keepdims=True)
            acc_ref[g] = alpha * acc_ref[g] + jnp.dot(p.astype(BF16), v_c, preferred_element_type=F32)
            m_ref[g] = m_new
        return carry

    lax.fori_loop(0, qt + 1, slc_chunk, 0)
    o_slc = [acc_ref[g] / l_ref[g] for g in range(NSA_GROUP)]

    start = pl.multiple_of(jnp.maximum(qt * tq - (wk - tq), 0), tq)
    kw = kw_ref[0, 0, pl.ds(start, wk), :]
    vw = vw_ref[0, 0, pl.ds(start, wk), :]
    diff = pos - (start + lax.broadcasted_iota(jnp.int32, (1, wk), 1))
    wmask = (diff >= 0) & (diff < WINDOW)
    for g in range(NSA_GROUP):
        s = _dot_nt(qr[g], kw)
        p = _softmax_rows(s, wmask)
        o_win = jnp.dot(p.astype(BF16), vw, preferred_element_type=F32)
        c = g * 3
        o = gates[:, c:c + 1] * o_cmp[g] + gates[:, c + 1:c + 2] * o_slc[g] + gates[:, c + 2:c + 3] * o_win
        o_ref[0, :, g * HEAD_DIM:(g + 1) * HEAD_DIM] = o.astype(o_ref.dtype)


def _selection_constants(S, tq):
    nc = S // CMP_STRIDE
    n_cmp = (S - CMP_BLOCK) // CMP_STRIDE + 1
    n_slc = S // SLC_BLOCK
    ratio = SLC_BLOCK // CMP_STRIDE
    span = CMP_BLOCK // CMP_STRIDE
    agg_w = np.convolve(np.ones(ratio), np.ones(span))
    agg = np.zeros((nc, n_slc), np.float32)
    for j in range(n_slc):
        for w in range(ratio + span - 1):
            c = ratio * j + w - (span - 1)
            if 0 <= c < n_cmp:
                agg[c, j] += agg_w[w]
    expand = (np.arange(S)[None, :] // SLC_BLOCK == np.arange(n_slc)[:, None]).astype(np.float32)
    expand = expand.reshape(n_slc, S // tq, tq).transpose(1, 0, 2)
    return jnp.asarray(agg.T), jnp.asarray(expand, dtype=BF16)


def nsa_attention(proj3, gate3, kc, vc, ks, vs, kw, vw, cos, sin, tq=256):
    B, S, _ = proj3.shape
    nc = S // CMP_STRIDE
    n_slc = S // SLC_BLOCK
    n_sel = min(N_SELECT, n_slc)
    wk = min(S, WINDOW + tq)
    agg, expand = _selection_constants(S, tq)
    qw = NSA_GROUP * HEAD_DIM
    cspec = pl.BlockSpec((1, 1, nc, HEAD_DIM), lambda b, h, t: (b, h, 0, 0))
    sspec = pl.BlockSpec((1, 1, S, HEAD_DIM), lambda b, h, t: (b, h, 0, 0))
    return pl.pallas_call(
        functools.partial(_nsa_attn_kernel, tq=tq, S=S, n_sel=n_sel, wk=wk),
        grid=(B, NSA_KV_HEADS, S // tq),
        in_specs=[
            pl.BlockSpec((1, tq, qw), lambda b, h, t: (b, t, COL_Q // qw + h)),
            pl.BlockSpec((1, tq, GATE_PAD), lambda b, h, t: (b, t, COL_GATE // GATE_PAD + h)),
            cspec, cspec, sspec, sspec, sspec, sspec,
            pl.BlockSpec((tq, HEAD_DIM), lambda b, h, t: (t, 0)),
            pl.BlockSpec((tq, HEAD_DIM), lambda b, h, t: (t, 0)),
            pl.BlockSpec((n_slc, nc), lambda b, h, t: (0, 0)),
            pl.BlockSpec((S // tq, n_slc, tq), lambda b, h, t: (0, 0, 0)),
        ],
        out_specs=pl.BlockSpec((1, tq, qw), lambda b, h, t: (b, t, h)),
        out_shape=jax.ShapeDtypeStruct((B, S, NSA_HEADS * HEAD_DIM), BF16),
        scratch_shapes=[pltpu.VMEM((NSA_GROUP, tq, HEAD_DIM), F32) for _ in range(3)],
        compiler_params=_params("parallel", "parallel", "parallel"),
        name="nsa_attention",
    )(proj3, gate3, kc, vc, ks, vs, kw, vw, cos, sin, agg, expand)


def _merge_out_kernel(ma_ref, mb_ref, oa_ref, ob_ref, x_ref, w_ref, g_ref, x1_ref, h_ref, hb_ref):
    m = (_sigmoid(ma_ref[...].astype(F32)) * oa_ref[...].astype(F32)
         + _sigmoid(mb_ref[...].astype(F32)) * ob_ref[...].astype(F32))
    x1 = x_ref[...] + jnp.dot(m.astype(BF16), w_ref[...], preferred_element_type=F32)
    x1_ref[...] = x1
    h = _rms(x1, g_ref[...])
    h_ref[...] = h
    hb_ref[...] = h.astype(hb_ref.dtype)


def merge_out(proj, o_a, o_b, x, w_out, g, tm=256):
    T, D = x.shape
    row = lambda c: pl.BlockSpec((tm, D), lambda i, c=c: (i, c))
    return pl.pallas_call(
        _merge_out_kernel,
        grid=(T // tm,),
        in_specs=[row(COL_MERGE_A // D), row(COL_MERGE_B // D), row(0), row(0), row(0),
                  pl.BlockSpec((D, D), lambda i: (0, 0)), pl.BlockSpec((1, D), lambda i: (0, 0))],
        out_specs=[row(0), row(0), row(0)],
        out_shape=[jax.ShapeDtypeStruct((T, D), F32), jax.ShapeDtypeStruct((T, D), F32),
                   jax.ShapeDtypeStruct((T, D), BF16)],
        compiler_params=_params("parallel"),
        name="merge_out",
    )(proj, proj, o_a, o_b, x, w_out, g.reshape(1, D))


def _topk_rows(vals, order, payload, k):
    big = float(2 ** 30)
    out_v, out_p = [], []
    for _ in range(k):
        m = jnp.max(vals, axis=0, keepdims=True)
        first = jnp.min(jnp.where(vals == m, order, big), axis=0, keepdims=True)
        hit = order == first
        out_v.append(m)
        out_p.append(first if payload is order else jnp.max(jnp.where(hit, payload, -1.0), axis=0, keepdims=True))
        vals = jnp.where(hit, -jnp.inf, vals)
    return jnp.concatenate(out_v, axis=0), jnp.concatenate(out_p, axis=0)


def _product_key_cells():
    pieces = []
    a = 0
    while a < PEER_TOPK and PEER_TOPK // (a + 1) > 1:
        n_b = PEER_TOPK // (a + 1)
        pieces.append((a, 1, n_b, -(-n_b // SUBLANE) * SUBLANE))
        a += 1
    pieces.append((a, PEER_TOPK - a, 1, 1))
    return pieces


def _peer_route_kernel(q_ref, k1_ref, k2_ref, e_ref, g_ref):
    tt = q_ref.shape[0]
    key_row = lax.broadcasted_iota(jnp.int32, (PEER_N_KEYS, tt), 0).astype(F32)
    pieces = _product_key_cells()
    flat = []
    for a0, n_a, n_b, rows in pieces:
        if n_a == 1:
            flat.append(a0 * PEER_TOPK + lax.broadcasted_iota(jnp.int32, (rows, tt), 0))
        else:
            flat.append((a0 + lax.broadcasted_iota(jnp.int32, (n_a, tt), 0)) * PEER_TOPK)
    flat = jnp.concatenate(flat, axis=0).astype(F32)
    experts, gates = [], []
    for h in range(PEER_HEADS):
        tops = []
        for c, k_ref in enumerate((k1_ref, k2_ref)):
            qh = q_ref[:, (2 * h + c) * PEER_N_KEYS:(2 * h + c + 1) * PEER_N_KEYS]
            s = _dot_nt(k_ref[...], qh, precision=lax.Precision.HIGHEST)
            tops.append(_topk_rows(s, key_row, key_row, PEER_TOPK))
        (v1, i1), (v2, i2) = tops
        cand, cand_id = [], []
        for a0, n_a, n_b, rows in pieces:
            if n_a == 1:
                vals = v1[a0:a0 + 1, :] + v2[:rows, :]
                if n_b < rows:
                    vals = jnp.where(lax.broadcasted_iota(jnp.int32, (rows, tt), 0) < n_b, vals, -jnp.inf)
                cand.append(vals)
                cand_id.append(i1[a0:a0 + 1, :] * PEER_N_KEYS + i2[:rows, :])
            else:
                cand.append(v1[a0:a0 + n_a, :] + v2[0:1, :])
                cand_id.append(i1[a0:a0 + n_a, :] * PEER_N_KEYS + i2[0:1, :])
        top_s, top_e = _topk_rows(jnp.concatenate(cand, axis=0), flat, jnp.concatenate(cand_id, axis=0), PEER_TOPK)
        ex = jnp.exp(top_s - jnp.max(top_s, axis=0, keepdims=True))
        experts.append(top_e)
        gates.append(ex / jnp.sum(ex, axis=0, keepdims=True))
    e_ref[...] = jnp.concatenate(experts, axis=0).T.astype(jnp.int32)
    gate = jnp.concatenate(gates, axis=0)
    tb = g_ref.shape[2]
    for j in range(tt // tb):
        g_ref[j] = gate[:, j * tb:(j + 1) * tb]


def peer_route(qp, keys1, keys2, tb, tt=128):
    T, Q = qp.shape
    kspec = pl.BlockSpec(keys1.shape, lambda i: (0, 0))
    return pl.pallas_call(
        _peer_route_kernel,
        grid=(T // tt,),
        in_specs=[pl.BlockSpec((tt, Q), lambda i: (i, 0)), kspec, kspec],
        out_specs=[pl.BlockSpec((tt, PEER_SLOTS), lambda i: (i, 0)),
                   pl.BlockSpec((tt // tb, PEER_SLOTS, tb), lambda i: (i, 0, 0))],
        out_shape=[jax.ShapeDtypeStruct((T, PEER_SLOTS), jnp.int32),
                   jax.ShapeDtypeStruct((T // tb, PEER_SLOTS, tb), F32)],
        compiler_params=_params("parallel"),
        name="peer_route",
    )(qp, keys1, keys2)


X_ROWS = D_MODEL // LANE
UV_ROWS = 2 * X_ROWS
PEER_NSLOT = 8
PEER_TOKEN_BLOCK = 64
DMA_PRIORITIES = 2


def _pack_uv(u, v):
    n_exp = u.shape[0]
    return jnp.concatenate([u.reshape(n_exp, X_ROWS, LANE), v.reshape(n_exp, X_ROWS, LANE)], axis=1).astype(BF16)


def _peer_apply_kernel(idx_ref, idx_next_ref, h_ref, gt_ref, uv_hbm, o_ref, *scratch, tb):
    bufs, p_ref, w_ref, sem = scratch[:PEER_NSLOT], scratch[PEER_NSLOT], scratch[PEER_NSLOT + 1], scratch[PEER_NSLOT + 2]
    ahead = PEER_NSLOT - 1
    step = pl.program_id(0)

    def row_copy(ids, t, j, slot, row):
        return pltpu.make_async_copy(uv_hbm.at[ids[t, j]], bufs[slot].at[pl.ds(row, UV_ROWS), :], sem.at[slot])

    def issue(ids, t, slot):
        for j in range(PEER_SLOTS):
            row_copy(ids, t, j, slot, j * UV_ROWS).start(priority=j % DMA_PRIORITIES)

    def issue_rolled(ids, t, slot):
        def body(j, carry):
            row_copy(ids, t, j, slot, pl.multiple_of(j * UV_ROWS, UV_ROWS)).start()
            return carry
        lax.fori_loop(0, PEER_SLOTS, body, 0)

    def wait_all(slot):
        pltpu.make_async_copy(bufs[(slot + 1) % PEER_NSLOT], bufs[slot], sem.at[slot]).wait()

    lane = lax.broadcasted_iota(jnp.int32, (PEER_SLOTS, tb), 1)

    def compute(t, slot):
        buf = bufs[slot]
        x = h_ref[t]
        for j in range(PEER_SLOTS):
            prod = buf[j * UV_ROWS:j * UV_ROWS + X_ROWS, :].astype(F32) * x
            p_ref[j * SUBLANE:(j + 1) * SUBLANE, :] = prod[:SUBLANE] + prod[SUBLANE:]
        acc = p_ref[pl.ds(0, PEER_SLOTS, stride=SUBLANE), :]
        for q in range(1, SUBLANE):
            acc = acc + p_ref[pl.ds(q, PEER_SLOTS, stride=SUBLANE), :]
        act = _gelu(jnp.sum(acc, axis=1, keepdims=True))
        gate = jnp.sum(jnp.where(lane == t, gt_ref[...], 0.0), axis=1, keepdims=True)
        w_ref[...] = jnp.broadcast_to(gate * act, (PEER_SLOTS, LANE))
        n_acc = 4
        out = [jnp.zeros((X_ROWS, LANE), F32) for _ in range(n_acc)]
        for j in range(PEER_SLOTS):
            base = j * UV_ROWS + X_ROWS
            out[j % n_acc] = out[j % n_acc] + buf[base:base + X_ROWS, :].astype(F32) * w_ref[j:j + 1, :]
        o_ref[t] = (out[0] + out[1]) + (out[2] + out[3])

    def group(g, carry):
        t0 = g * PEER_NSLOT
        for i in range(PEER_NSLOT):
            wait_all(i)
            issue(idx_ref, t0 + i + ahead, (i + ahead) % PEER_NSLOT)
            compute(t0 + i, i)
        return carry

    @pl.when(step == 0)
    def _():
        for t in range(ahead):
            issue_rolled(idx_ref, t, t)

    n_groups = tb // PEER_NSLOT
    lax.fori_loop(0, n_groups - 1, group, 0)
    t0 = (n_groups - 1) * PEER_NSLOT
    for i in range(PEER_NSLOT):
        wait_all(i)
        if i + ahead < PEER_NSLOT:
            issue(idx_ref, t0 + i + ahead, (i + ahead) % PEER_NSLOT)
        else:
            @pl.when(step + 1 < pl.num_programs(0))
            def _(i=i):
                issue(idx_next_ref, i + ahead - PEER_NSLOT, (i + ahead) % PEER_NSLOT)
        compute(t0 + i, i)


def peer_apply(experts, h, gate_blocks, uv):
    T, D = h.shape
    n_blocks, _, tb = gate_blocks.shape
    tok_spec = pl.BlockSpec((tb, X_ROWS, LANE), lambda i: (i, 0, 0))
    y = pl.pallas_call(
        functools.partial(_peer_apply_kernel, tb=tb),
        grid=(n_blocks,),
        in_specs=[
            pl.BlockSpec((tb, PEER_SLOTS), lambda i: (i, 0), memory_space=pltpu.SMEM),
            pl.BlockSpec((tb, PEER_SLOTS), lambda i: (jnp.minimum(i + 1, n_blocks - 1), 0), memory_space=pltpu.SMEM),
            tok_spec,
            pl.BlockSpec((None, PEER_SLOTS, tb), lambda i: (i, 0, 0)),
            pl.BlockSpec(memory_space=pl.ANY),
        ],
        out_specs=tok_spec,
        out_shape=jax.ShapeDtypeStruct((T, X_ROWS, LANE), F32),
        scratch_shapes=[pltpu.VMEM((PEER_SLOTS * UV_ROWS, LANE), BF16) for _ in range(PEER_NSLOT)] + [
            pltpu.VMEM((PEER_SLOTS * SUBLANE, LANE), F32), pltpu.VMEM((PEER_SLOTS, LANE), F32),
            pltpu.SemaphoreType.DMA((PEER_NSLOT,))],
        compiler_params=_params("arbitrary"),
        name="peer_apply",
    )(experts, experts, h.reshape(T, X_ROWS, LANE), gate_blocks, uv)
    return y.reshape(T, D)


def _final_kernel(x_ref, y_ref, g_ref, o_ref):
    o_ref[...] = _rms(x_ref[...] + y_ref[...], g_ref[...])


def final_norm(x1, y, g, tm=512):
    T, D = x1.shape
    spec = pl.BlockSpec((tm, D), lambda i: (i, 0))
    return pl.pallas_call(
        _final_kernel,
        grid=(T // tm,),
        in_specs=[spec, spec, pl.BlockSpec((1, D), lambda i: (0, 0))],
        out_specs=spec,
        out_shape=jax.ShapeDtypeStruct((T, D), F32),
        compiler_params=_params("parallel"),
        name="final_norm",
    )(x1, y, g.reshape(1, D))


def _pack_w_in(w):
    u, v, q, kv, gate, merge = jnp.split(w, [2048, 4096, 6144, 9216, 9264], axis=1)
    gate = gate.reshape(D_MODEL, NSA_KV_HEADS, NSA_GROUP * 3)
    gate = jnp.pad(gate, ((0, 0), (0, 0), (0, GATE_PAD - NSA_GROUP * 3))).reshape(D_MODEL, NSA_KV_HEADS * GATE_PAD)
    part_a = jnp.concatenate([u, v, q, merge], axis=1).astype(BF16)
    part_b = jnp.concatenate([kv, gate], axis=1).astype(BF16)
    return part_a, part_b


def _rope_tables(S):
    half = HEAD_DIM // 2
    inv = ROPE_THETA ** (-jnp.arange(half, dtype=F32) / half)
    ang = jnp.arange(S, dtype=F32)[:, None] * inv[None, :]
    cos = jnp.cos(ang)
    sin = jnp.sin(ang)
    return jnp.concatenate([cos, cos], axis=1), jnp.concatenate([-sin, sin], axis=1)


def _layer(x, norm_mix_g, w_in, w_out, gm_ln_g, gm_ln_b, gm_spatial_w, gm_spatial_b,
           cmp_pos_k, cmp_w1_k, cmp_w2_k, cmp_pos_v, cmp_w1_v, cmp_w2_v,
           norm_ffn_g, peer_w_q, peer_keys1, peer_keys2, peer_u, peer_v):
    B, S, D = x.shape
    T = B * S
    xt = x.reshape(T, D)
    h = rmsnorm_cast(xt, norm_mix_g)
    w_a, w_b = _pack_w_in(w_in)
    proj = matmul(h, w_a, 1024, 2048, BF16, "in_proj_a")
    proj_b = matmul(h, w_b, 1024, COL_B_TOTAL // 2, F32, "in_proj_b")
    o_a = gmlp_mixer(proj, gm_ln_g, gm_ln_b, gm_spatial_w, gm_spatial_b)
    proj3 = proj.reshape(B, S, COL_A_TOTAL)
    proj_b3 = proj_b.reshape(B, S, COL_B_TOTAL)
    cos, sin = _rope_tables(S)
    kc, vc, ks, vs, kw, vw = nsa_prep(proj_b3, cos, sin, cmp_pos_k, cmp_w1_k, cmp_w2_k, cmp_pos_v, cmp_w1_v, cmp_w2_v)
    o_b = nsa_attention(proj3, proj_b3, kc, vc, ks, vs, kw, vw, cos, sin).reshape(T, D)
    x1, h2, h2b = merge_out(proj, o_a, o_b, xt, w_out.astype(BF16), norm_ffn_g)
    qp = matmul(h2b, peer_w_q.astype(BF16), 1024, 1024, F32, "peer_query")
    experts, gate_blocks = peer_route(qp, peer_keys1, peer_keys2, PEER_TOKEN_BLOCK)
    y = peer_apply(experts, h2, gate_blocks, _pack_uv(peer_u, peer_v))
    return x1, y


def kernel(x, norm_mix_g, w_in, w_out, gm_ln_g, gm_ln_b, gm_spatial_w, gm_spatial_b, cmp_pos_k, cmp_w1_k, cmp_w2_k, cmp_pos_v, cmp_w1_v, cmp_w2_v, norm_ffn_g, peer_w_q, peer_keys1, peer_keys2, peer_u, peer_v, norm_final_g):
    B, S, D = x.shape
    depth = w_in.shape[0]
    y = None
    for l in range(depth):
        if y is not None:
            x = (x.reshape(B * S, D) + y).reshape(B, S, D)
        x1, y = _layer(x, norm_mix_g[l], w_in[l], w_out[l], gm_ln_g[l], gm_ln_b[l], gm_spatial_w[l], gm_spatial_b[l],
                       cmp_pos_k[l], cmp_w1_k[l], cmp_w2_k[l], cmp_pos_v[l], cmp_w1_v[l], cmp_w2_v[l],
                       norm_ffn_g[l], peer_w_q[l], peer_keys1[l], peer_keys2[l], peer_u[l], peer_v[l])
        x = x1.reshape(B, S, D)
    return final_norm(x.reshape(B * S, D), y, norm_final_g).reshape(B, S, D)
```

```python
import functools
import math

import numpy as np
import jax
import jax.numpy as jnp
from jax import lax
from jax.experimental import pallas as pl
from jax.experimental.pallas import tpu as pltpu

D_MODEL = 2048
GM_GROUPS = 16
GM_GROUP_DIM = 128
GM_CHUNK = 128

NSA_HEADS = 16
NSA_KV_HEADS = 4
NSA_GROUP = 4
HEAD_DIM = 128
CMP_BLOCK = 32
CMP_STRIDE = 16
CMP_HIDDEN = 256
SLC_BLOCK = 64
N_SELECT = 16
WINDOW = 512
ROPE_THETA = 10000.0

PEER_HEADS = 8
PEER_N_KEYS = 128
PEER_TOPK = 16
PEER_SLOTS = PEER_HEADS * PEER_TOPK

EPS = 1e-6
NEG_BIG = -1e30
POS_BIG = 1e30

VMEM_LIMIT_BYTES = 56 * 1024 * 1024
LANE = 128
SUBLANE = 8

COL_U = 0
COL_V = 2048
COL_Q = 4096
COL_MERGE_A = 6144
COL_MERGE_B = 8192
COL_A_TOTAL = 10240
COL_KV = 0
COL_GATE = 3072
GATE_PAD = 128
COL_B_TOTAL = COL_GATE + NSA_KV_HEADS * GATE_PAD

BF16 = jnp.bfloat16
F32 = jnp.float32


def _params(*sem):
    return pltpu.CompilerParams(dimension_semantics=sem, vmem_limit_bytes=VMEM_LIMIT_BYTES)


def _gelu(x):
    return 0.5 * x * (1.0 + jnp.tanh(math.sqrt(2.0 / math.pi) * (x + 0.044715 * (x * x * x))))


def _sigmoid(x):
    return 1.0 / (1.0 + jnp.exp(-x))


def _rms(x, g):
    return x * lax.rsqrt(jnp.mean(x * x, axis=-1, keepdims=True) + EPS) * g


def _dot_nt(a, b, **kw):
    return lax.dot_general(a, b, (((1,), (1,)), ((), ())), preferred_element_type=F32, **kw)


def _rmsnorm_kernel(x_ref, g_ref, o_ref):
    o_ref[...] = _rms(x_ref[...], g_ref[...]).astype(o_ref.dtype)


def rmsnorm_cast(x, g, tm=512):
    T, D = x.shape
    return pl.pallas_call(
        _rmsnorm_kernel,
        grid=(T // tm,),
        in_specs=[pl.BlockSpec((tm, D), lambda i: (i, 0)), pl.BlockSpec((1, D), lambda i: (0, 0))],
        out_specs=pl.BlockSpec((tm, D), lambda i: (i, 0)),
        out_shape=jax.ShapeDtypeStruct((T, D), BF16),
        compiler_params=_params("parallel"),
        name="rmsnorm_cast",
    )(x, g.reshape(1, D))


def _mm_kernel(a_ref, w_ref, o_ref):
    o_ref[...] = jnp.dot(a_ref[...], w_ref[...], preferred_element_type=F32).astype(o_ref.dtype)


def matmul(a, w, tm, tn, out_dtype, name):
    M, K = a.shape
    N = w.shape[1]
    return pl.pallas_call(
        _mm_kernel,
        grid=(N // tn, M // tm),
        in_specs=[pl.BlockSpec((tm, K), lambda j, i: (i, 0)), pl.BlockSpec((K, tn), lambda j, i: (0, j))],
        out_specs=pl.BlockSpec((tm, tn), lambda j, i: (i, j)),
        out_shape=jax.ShapeDtypeStruct((M, N), out_dtype),
        compiler_params=_params("parallel", "parallel"),
        name=name,
    )(a, w)


def _gmlp_kernel(u_ref, v_ref, g_ref, b_ref, w_ref, bs_ref, o_ref):
    gv = _gelu(v_ref[...].astype(F32))
    mu = jnp.mean(gv, axis=-1, keepdims=True)
    xc = gv - mu
    vn = xc * lax.rsqrt(jnp.mean(xc * xc, axis=-1, keepdims=True) + EPS) * g_ref[...] + b_ref[...]
    vn = vn.astype(BF16)
    row = lax.broadcasted_iota(jnp.int32, (GM_CHUNK, GM_CHUNK), 0)
    col = lax.broadcasted_iota(jnp.int32, (GM_CHUNK, GM_CHUNK), 1)
    causal = row >= col
    bs = bs_ref[...]
    for g in range(GM_GROUPS):
        sl = slice(g * GM_GROUP_DIM, (g + 1) * GM_GROUP_DIM)
        w = jnp.where(causal, w_ref[g], 0.0).astype(BF16)
        z = jnp.dot(w, vn[:, sl], preferred_element_type=F32) + bs[:, g:g + 1]
        o_ref[:, sl] = (_gelu(u_ref[:, sl].astype(F32)) * z).astype(o_ref.dtype)


def gmlp_mixer(proj, ln_g, ln_b, w_s, b_s):
    T = proj.shape[0]
    W = GM_GROUPS * GM_GROUP_DIM
    return pl.pallas_call(
        _gmlp_kernel,
        grid=(T // GM_CHUNK,),
        in_specs=[
            pl.BlockSpec((GM_CHUNK, W), lambda i: (i, COL_U // W)),
            pl.BlockSpec((GM_CHUNK, W), lambda i: (i, COL_V // W)),
            pl.BlockSpec((1, W), lambda i: (0, 0)),
            pl.BlockSpec((1, W), lambda i: (0, 0)),
            pl.BlockSpec((GM_GROUPS, GM_CHUNK, GM_CHUNK), lambda i: (0, 0, 0)),
            pl.BlockSpec((GM_CHUNK, GM_GROUPS), lambda i: (0, 0)),
        ],
        out_specs=pl.BlockSpec((GM_CHUNK, W), lambda i: (i, 0)),
        out_shape=jax.ShapeDtypeStruct((T, W), BF16),
        compiler_params=_params("parallel"),
        name="gmlp_mixer",
    )(proj, proj, ln_g.reshape(1, W), ln_b.reshape(1, W), w_s, b_s.T)


def _rope(x, cos, sin_signed):
    return x * cos + pltpu.roll(x, HEAD_DIM // 2, axis=1) * sin_signed


def _compress(k_ref, pos_ref, w1_ref, w2_ref, nc):
    half = CMP_BLOCK // 2
    a = jnp.zeros((nc, CMP_HIDDEN), F32)
    b = jnp.zeros((nc, CMP_HIDDEN), F32)
    for l in range(half):
        kl = k_ref[0, pl.ds(l, nc, stride=CMP_STRIDE), :]
        a = a + jnp.dot((kl + pos_ref[l:l + 1, :]).astype(BF16), w1_ref[l], preferred_element_type=F32)
        b = b + jnp.dot((kl + pos_ref[half + l:half + l + 1, :]).astype(BF16), w1_ref[half + l],
                        preferred_element_type=F32)
    hid = a + pltpu.roll(b, nc - 1, axis=0)
    return jnp.dot(_gelu(hid).astype(BF16), w2_ref[...], preferred_element_type=F32)


def _nsa_prep_kernel(kc_ref, vc_ref, ks_ref, vs_ref, kw_ref, vw_ref, cos_ref, sin_ref,
                     pk_ref, w1k_ref, w2k_ref, pv_ref, w1v_ref, w2v_ref,
                     okc_ref, ovc_ref, oks_ref, ovs_ref, okw_ref, ovw_ref, *, nc):
    okc_ref[0, 0] = _compress(kc_ref, pk_ref, w1k_ref, w2k_ref, nc).astype(okc_ref.dtype)
    ovc_ref[0, 0] = _compress(vc_ref, pv_ref, w1v_ref, w2v_ref, nc).astype(ovc_ref.dtype)
    cos = cos_ref[...]
    sin = sin_ref[...]
    oks_ref[0, 0] = _rope(ks_ref[0], cos, sin).astype(oks_ref.dtype)
    okw_ref[0, 0] = _rope(kw_ref[0], cos, sin).astype(okw_ref.dtype)
    ovs_ref[0, 0] = vs_ref[0].astype(ovs_ref.dtype)
    ovw_ref[0, 0] = vw_ref[0].astype(ovw_ref.dtype)


def nsa_prep(proj3, cos, sin, pos_k, w1_k, w2_k, pos_v, w1_v, w2_v):
    B, S, _ = proj3.shape
    nc = S // CMP_STRIDE
    kvb = COL_KV // HEAD_DIM

    def kv_spec(i):
        return pl.BlockSpec((1, S, HEAD_DIM), lambda b, h, i=i: (b, 0, kvb + i * NSA_KV_HEADS + h))

    full2 = lambda shp: pl.BlockSpec(shp, lambda b, h: (0, 0))
    full3 = lambda shp: pl.BlockSpec(shp, lambda b, h: (0, 0, 0))
    out_c = pl.BlockSpec((1, 1, nc, HEAD_DIM), lambda b, h: (b, h, 0, 0))
    out_s = pl.BlockSpec((1, 1, S, HEAD_DIM), lambda b, h: (b, h, 0, 0))
    shp_c = jax.ShapeDtypeStruct((B, NSA_KV_HEADS, nc, HEAD_DIM), BF16)
    shp_s = jax.ShapeDtypeStruct((B, NSA_KV_HEADS, S, HEAD_DIM), BF16)
    w1k = w1_k.astype(BF16).reshape(CMP_BLOCK, HEAD_DIM, CMP_HIDDEN)
    w1v = w1_v.astype(BF16).reshape(CMP_BLOCK, HEAD_DIM, CMP_HIDDEN)
    return pl.pallas_call(
        functools.partial(_nsa_prep_kernel, nc=nc),
        grid=(B, NSA_KV_HEADS),
        in_specs=[kv_spec(i) for i in range(6)] + [
            full2((S, HEAD_DIM)), full2((S, HEAD_DIM)),
            full2((CMP_BLOCK, HEAD_DIM)), full3((CMP_BLOCK, HEAD_DIM, CMP_HIDDEN)), full2((CMP_HIDDEN, HEAD_DIM)),
            full2((CMP_BLOCK, HEAD_DIM)), full3((CMP_BLOCK, HEAD_DIM, CMP_HIDDEN)), full2((CMP_HIDDEN, HEAD_DIM)),
        ],
        out_specs=[out_c, out_c, out_s, out_s, out_s, out_s],
        out_shape=[shp_c, shp_c, shp_s, shp_s, shp_s, shp_s],
        compiler_params=_params("parallel", "parallel"),
        name="nsa_prep",
    )(proj3, proj3, proj3, proj3, proj3, proj3, cos, sin,
      pos_k, w1k, w2_k.astype(BF16), pos_v, w1v, w2_v.astype(BF16))


def _softmax_rows(s, mask):
    s = jnp.where(mask, s, NEG_BIG)
    m = jnp.max(s, axis=-1, keepdims=True)
    e = jnp.exp(s - m)
    return e / jnp.sum(e, axis=-1, keepdims=True)


def _nsa_attn_kernel(q_ref, gate_ref, kc_ref, vc_ref, ks_ref, vs_ref, kw_ref, vw_ref,
                     cos_ref, sin_ref, agg_ref, exp_ref, o_ref, m_ref, l_ref, acc_ref, *, tq, S, n_sel, wk):
    qt = pl.program_id(2)
    nc = S // CMP_STRIDE
    n_slc = S // SLC_BLOCK
    scale = HEAD_DIM ** -0.5
    pos = qt * tq + lax.broadcasted_iota(jnp.int32, (tq, 1), 0)
    cos = cos_ref[...]
    sin = sin_ref[...]
    gates = _sigmoid(gate_ref[0])

    qs = [q_ref[0, :, g * HEAD_DIM:(g + 1) * HEAD_DIM].astype(F32) for g in range(NSA_GROUP)]
    qr = [(_rope(q, cos, sin) * scale).astype(BF16) for q in qs]

    kc = kc_ref[0, 0]
    vc = vc_ref[0, 0]
    pos_t = qt * tq + lax.broadcasted_iota(jnp.int32, (1, tq), 1)
    cmp_end = lax.broadcasted_iota(jnp.int32, (nc, 1), 0) * CMP_STRIDE + (CMP_BLOCK - 1)
    cmask = (cmp_end <= pos_t) & (cmp_end < S)
    any_c = pos_t >= CMP_BLOCK - 1
    o_cmp = []
    imp = jnp.zeros((nc, tq), F32)
    for g in range(NSA_GROUP):
        s = jnp.where(cmask, _dot_nt(kc, (qs[g] * scale).astype(BF16)), NEG_BIG)
        e = jnp.exp(s - jnp.max(s, axis=0, keepdims=True))
        p = jnp.where(any_c, e / jnp.sum(e, axis=0, keepdims=True), 0.0)
        o_cmp.append(jnp.dot(p.T.astype(BF16), vc, preferred_element_type=F32))
        imp = imp + p

    imp_slc = jnp.dot(agg_ref[...], imp, preferred_element_type=F32, precision=lax.Precision.HIGHEST)
    blk = lax.broadcasted_iota(jnp.int32, (n_slc, tq), 0)
    allowed = blk * SLC_BLOCK <= pos_t
    forced = (blk == 0) | (blk == pos_t // SLC_BLOCK)
    score = jnp.where(forced, POS_BIG, jnp.where(allowed, imp_slc, NEG_BIG))
    rank = jnp.zeros((n_slc, tq), F32)
    for i in range(n_slc):
        si = score[i:i + 1, :]
        tie_before = jnp.where(blk > i, 1.0, 0.0)
        rank = rank + jnp.where(si > score, 1.0, jnp.where(si == score, tie_before, 0.0))
    sel_t = jnp.where(rank < n_sel, 1.0, 0.0)
    pad_rows = LANE - n_slc
    sel = jnp.concatenate([sel_t, jnp.zeros((pad_rows, tq), F32)], axis=0).T[:, :n_slc].astype(BF16)

    m_ref[...] = jnp.full(m_ref.shape, NEG_BIG, F32)
    l_ref[...] = jnp.zeros(l_ref.shape, F32)
    acc_ref[...] = jnp.zeros(acc_ref.shape, F32)

    def slc_chunk(c, carry):
        off = pl.multiple_of(c * tq, tq)
        k_c = ks_ref[0, 0, pl.ds(off, tq), :]
        v_c = vs_ref[0, 0, pl.ds(off, tq), :]
        in_sel = jnp.dot(sel, exp_ref[c], preferred_element_type=F32) > 0.5
        valid = in_sel & (off + lax.broadcasted_iota(jnp.int32, (1, tq), 1) <= pos)
        for g in range(NSA_GROUP):
            s = jnp.where(valid, _dot_nt(qr[g], k_c), NEG_BIG)
            m_old = m_ref[g]
            m_new = jnp.maximum(m_old, jnp.max(s, axis=-1, keepdims=True))
            alpha = jnp.exp(m_old - m_new)
            p = jnp.exp(s - jnp.concatenate([m_new] * (tq // LANE), axis=1))
            l_ref[g] = alpha * l_ref[g] + jnp.sum(p, axis=-1, keepdims=True)
            acc_ref[g] = alpha * acc_ref[g] + jnp.dot(p.astype(BF16), v_c, preferred_element_type=F32)
            m_ref[g] = m_new
        return carry

    lax.fori_loop(0, qt + 1, slc_chunk, 0)
    o_slc = [acc_ref[g] / l_ref[g] for g in range(NSA_GROUP)]

    start = pl.multiple_of(jnp.maximum(qt * tq - (wk - tq), 0), tq)
    kw = kw_ref[0, 0, pl.ds(start, wk), :]
    vw = vw_ref[0, 0, pl.ds(start, wk), :]
    diff = pos - (start + lax.broadcasted_iota(jnp.int32, (1, wk), 1))
    wmask = (diff >= 0) & (diff < WINDOW)
    for g in range(NSA_GROUP):
        s = _dot_nt(qr[g], kw)
        p = _softmax_rows(s, wmask)
        o_win = jnp.dot(p.astype(BF16), vw, preferred_element_type=F32)
        c = g * 3
        o = gates[:, c:c + 1] * o_cmp[g] + gates[:, c + 1:c + 2] * o_slc[g] + gates[:, c + 2:c + 3] * o_win
        o_ref[0, :, g * HEAD_DIM:(g + 1) * HEAD_DIM] = o.astype(o_ref.dtype)


def _selection_constants(S, tq):
    nc = S // CMP_STRIDE
    n_cmp = (S - CMP_BLOCK) // CMP_STRIDE + 1
    n_slc = S // SLC_BLOCK
    ratio = SLC_BLOCK // CMP_STRIDE
    span = CMP_BLOCK // CMP_STRIDE
    agg_w = np.convolve(np.ones(ratio), np.ones(span))
    agg = np.zeros((nc, n_slc), np.float32)
    for j in range(n_slc):
        for w in range(ratio + span - 1):
            c = ratio * j + w - (span - 1)
            if 0 <= c < n_cmp:
                agg[c, j] += agg_w[w]
    expand = (np.arange(S)[None, :] // SLC_BLOCK == np.arange(n_slc)[:, None]).astype(np.float32)
    expand = expand.reshape(n_slc, S // tq, tq).transpose(1, 0, 2)
    return jnp.asarray(agg.T), jnp.asarray(expand, dtype=BF16)


def nsa_attention(proj3, gate3, kc, vc, ks, vs, kw, vw, cos, sin, tq=256):
    B, S, _ = proj3.shape
    nc = S // CMP_STRIDE
    n_slc = S // SLC_BLOCK
    n_sel = min(N_SELECT, n_slc)
    wk = min(S, WINDOW + tq)
    agg, expand = _selection_constants(S, tq)
    qw = NSA_GROUP * HEAD_DIM
    cspec = pl.BlockSpec((1, 1, nc, HEAD_DIM), lambda b, h, t: (b, h, 0, 0))
    sspec = pl.BlockSpec((1, 1, S, HEAD_DIM), lambda b, h, t: (b, h, 0, 0))
    return pl.pallas_call(
        functools.partial(_nsa_attn_kernel, tq=tq, S=S, n_sel=n_sel, wk=wk),
        grid=(B, NSA_KV_HEADS, S // tq),
        in_specs=[
            pl.BlockSpec((1, tq, qw), lambda b, h, t: (b, t, COL_Q // qw + h)),
            pl.BlockSpec((1, tq, GATE_PAD), lambda b, h, t: (b, t, COL_GATE // GATE_PAD + h)),
            cspec, cspec, sspec, sspec, sspec, sspec,
            pl.BlockSpec((tq, HEAD_DIM), lambda b, h, t: (t, 0)),
            pl.BlockSpec((tq, HEAD_DIM), lambda b, h, t: (t, 0)),
            pl.BlockSpec((n_slc, nc), lambda b, h, t: (0, 0)),
            pl.BlockSpec((S // tq, n_slc, tq), lambda b, h, t: (0, 0, 0)),
        ],
        out_specs=pl.BlockSpec((1, tq, qw), lambda b, h, t: (b, t, h)),
        out_shape=jax.ShapeDtypeStruct((B, S, NSA_HEADS * HEAD_DIM), BF16),
        scratch_shapes=[pltpu.VMEM((NSA_GROUP, tq, HEAD_DIM), F32) for _ in range(3)],
        compiler_params=_params("parallel", "parallel", "parallel"),
        name="nsa_attention",
    )(proj3, gate3, kc, vc, ks, vs, kw, vw, cos, sin, agg, expand)


def _merge_out_kernel(ma_ref, mb_ref, oa_ref, ob_ref, x_ref, w_ref, g_ref, x1_ref, h_ref, hb_ref):
    m = (_sigmoid(ma_ref[...].astype(F32)) * oa_ref[...].astype(F32)
         + _sigmoid(mb_ref[...].astype(F32)) * ob_ref[...].astype(F32))
    x1 = x_ref[...] + jnp.dot(m.astype(BF16), w_ref[...], preferred_element_type=F32)
    x1_ref[...] = x1
    h = _rms(x1, g_ref[...])
    h_ref[...] = h
    hb_ref[...] = h.astype(hb_ref.dtype)


def merge_out(proj, o_a, o_b, x, w_out, g, tm=256):
    T, D = x.shape
    row = lambda c: pl.BlockSpec((tm, D), lambda i, c=c: (i, c))
    return pl.pallas_call(
        _merge_out_kernel,
        grid=(T // tm,),
        in_specs=[row(COL_MERGE_A // D), row(COL_MERGE_B // D), row(0), row(0), row(0),
                  pl.BlockSpec((D, D), lambda i: (0, 0)), pl.BlockSpec((1, D), lambda i: (0, 0))],
        out_specs=[row(0), row(0), row(0)],
        out_shape=[jax.ShapeDtypeStruct((T, D), F32), jax.ShapeDtypeStruct((T, D), F32),
                   jax.ShapeDtypeStruct((T, D), BF16)],
        compiler_params=_params("parallel"),
        name="merge_out",
    )(proj, proj, o_a, o_b, x, w_out, g.reshape(1, D))


def _topk_rows(vals, order, payload, k):
    big = float(2 ** 30)
    out_v, out_p = [], []
    for _ in range(k):
        m = jnp.max(vals, axis=0, keepdims=True)
        first = jnp.min(jnp.where(vals == m, order, big), axis=0, keepdims=True)
        hit = order == first
        out_v.append(m)
        out_p.append(first if payload is order else jnp.max(jnp.where(hit, payload, -1.0), axis=0, keepdims=True))
        vals = jnp.where(hit, -jnp.inf, vals)
    return jnp.concatenate(out_v, axis=0), jnp.concatenate(out_p, axis=0)


def _product_key_cells():
    pieces = []
    a = 0
    while a < PEER_TOPK and PEER_TOPK // (a + 1) > 1:
        n_b = PEER_TOPK // (a + 1)
        pieces.append((a, 1, n_b, -(-n_b // SUBLANE) * SUBLANE))
        a += 1
    pieces.append((a, PEER_TOPK - a, 1, 1))
    return pieces


def _peer_route_kernel(q_ref, k1_ref, k2_ref, e_ref, g_ref):
    tt = q_ref.shape[0]
    key_row = lax.broadcasted_iota(jnp.int32, (PEER_N_KEYS, tt), 0).astype(F32)
    pieces = _product_key_cells()
    flat = []
    for a0, n_a, n_b, rows in pieces:
        if n_a == 1:
            flat.append(a0 * PEER_TOPK + lax.broadcasted_iota(jnp.int32, (rows, tt), 0))
        else:
            flat.append((a0 + lax.broadcasted_iota(jnp.int32, (n_a, tt), 0)) * PEER_TOPK)
    flat = jnp.concatenate(flat, axis=0).astype(F32)
    experts, gates = [], []
    for h in range(PEER_HEADS):
        tops = []
        for c, k_ref in enumerate((k1_ref, k2_ref)):
            qh = q_ref[:, (2 * h + c) * PEER_N_KEYS:(2 * h + c + 1) * PEER_N_KEYS]
            s = _dot_nt(k_ref[...], qh, precision=lax.Precision.HIGHEST)
            tops.append(_topk_rows(s, key_row, key_row, PEER_TOPK))
        (v1, i1), (v2, i2) = tops
        cand, cand_id = [], []
        for a0, n_a, n_b, rows in pieces:
            if n_a == 1:
                vals = v1[a0:a0 + 1, :] + v2[:rows, :]
                if n_b < rows:
                    vals = jnp.where(lax.broadcasted_iota(jnp.int32, (rows, tt), 0) < n_b, vals, -jnp.inf)
                cand.append(vals)
                cand_id.append(i1[a0:a0 + 1, :] * PEER_N_KEYS + i2[:rows, :])
            else:
                cand.append(v1[a0:a0 + n_a, :] + v2[0:1, :])
                cand_id.append(i1[a0:a0 + n_a, :] * PEER_N_KEYS + i2[0:1, :])
        top_s, top_e = _topk_rows(jnp.concatenate(cand, axis=0), flat, jnp.concatenate(cand_id, axis=0), PEER_TOPK)
        ex = jnp.exp(top_s - jnp.max(top_s, axis=0, keepdims=True))
        experts.append(top_e)
        gates.append(ex / jnp.sum(ex, axis=0, keepdims=True))
    e_ref[...] = jnp.concatenate(experts, axis=0).T.astype(jnp.int32)
    gate = jnp.concatenate(gates, axis=0)
    tb = g_ref.shape[2]
    for j in range(tt // tb):
        g_ref[j] = gate[:, j * tb:(j + 1) * tb]


def peer_route(qp, keys1, keys2, tb, tt=128):
    T, Q = qp.shape
    kspec = pl.BlockSpec(keys1.shape, lambda i: (0, 0))
    return pl.pallas_call(
        _peer_route_kernel,
        grid=(T // tt,),
        in_specs=[pl.BlockSpec((tt, Q), lambda i: (i, 0)), kspec, kspec],
        out_specs=[pl.BlockSpec((tt, PEER_SLOTS), lambda i: (i, 0)),
                   pl.BlockSpec((tt // tb, PEER_SLOTS, tb), lambda i: (i, 0, 0))],
        out_shape=[jax.ShapeDtypeStruct((T, PEER_SLOTS), jnp.int32),
                   jax.ShapeDtypeStruct((T // tb, PEER_SLOTS, tb), F32)],
        compiler_params=_params("parallel"),
        name="peer_route",
    )(qp, keys1, keys2)


X_ROWS = D_MODEL // LANE
UV_ROWS = 2 * X_ROWS
PEER_NSLOT = 8
PEER_TOKEN_BLOCK = 64
DMA_PRIORITIES = 2


def _pack_uv(u, v):
    n_exp = u.shape[0]
    return jnp.concatenate([u.reshape(n_exp, X_ROWS, LANE), v.reshape(n_exp, X_ROWS, LANE)], axis=1).astype(BF16)


def _peer_apply_kernel(idx_ref, idx_next_ref, h_ref, x1_ref, gn_ref, gt_ref, uv_hbm, o_ref, *scratch, tb, normalize):
    bufs = scratch[:PEER_NSLOT]
    p_ref, w_ref, xg_ref, og_ref, sem = scratch[PEER_NSLOT:]
    ahead = PEER_NSLOT - 1
    step = pl.program_id(0)

    def row_copy(ids, t, j, slot, row):
        return pltpu.make_async_copy(uv_hbm.at[ids[t, j]], bufs[slot].at[pl.ds(row, UV_ROWS), :], sem.at[slot])

    def issue(ids, t, slot):
        for j in range(PEER_SLOTS):
            row_copy(ids, t, j, slot, j * UV_ROWS).start(priority=j % DMA_PRIORITIES)

    def issue_rolled(ids, t, slot):
        def body(j, carry):
            row_copy(ids, t, j, slot, pl.multiple_of(j * UV_ROWS, UV_ROWS)).start()
            return carry
        lax.fori_loop(0, PEER_SLOTS, body, 0)

    def wait_all(slot):
        pltpu.make_async_copy(bufs[(slot + 1) % PEER_NSLOT], bufs[slot], sem.at[slot]).wait()

    lane = lax.broadcasted_iota(jnp.int32, (PEER_SLOTS, tb), 1)

    def compute(t, slot):
        buf = bufs[slot]
        x = jnp.concatenate([xg_ref[slot:slot + 1, c * LANE:(c + 1) * LANE] for c in range(X_ROWS)], axis=0)
        for j in range(PEER_SLOTS):
            prod = buf[j * UV_ROWS:j * UV_ROWS + X_ROWS, :].astype(F32) * x
            p_ref[j * SUBLANE:(j + 1) * SUBLANE, :] = prod[:SUBLANE] + prod[SUBLANE:]
        acc = p_ref[pl.ds(0, PEER_SLOTS, stride=SUBLANE), :]
        for q in range(1, SUBLANE):
            acc = acc + p_ref[pl.ds(q, PEER_SLOTS, stride=SUBLANE), :]
        act = _gelu(jnp.sum(acc, axis=1, keepdims=True))
        gate = jnp.sum(jnp.where(lane == t, gt_ref[...], 0.0), axis=1, keepdims=True)
        w_ref[...] = jnp.broadcast_to(gate * act, (PEER_SLOTS, LANE))
        n_acc = 4
        out = [jnp.zeros((X_ROWS, LANE), F32) for _ in range(n_acc)]
        for j in range(PEER_SLOTS):
            base = j * UV_ROWS + X_ROWS
            out[j % n_acc] = out[j % n_acc] + buf[base:base + X_ROWS, :].astype(F32) * w_ref[j:j + 1, :]
        y = (out[0] + out[1]) + (out[2] + out[3])
        for c in range(X_ROWS):
            og_ref[slot:slot + 1, c * LANE:(c + 1) * LANE] = y[c:c + 1, :]

    def load_group(t0):
        xg_ref[...] = h_ref[pl.ds(t0, PEER_NSLOT), :]

    def store_group(t0):
        z = x1_ref[pl.ds(t0, PEER_NSLOT), :] + og_ref[...]
        o_ref[pl.ds(t0, PEER_NSLOT), :] = _rms(z, gn_ref[...]) if normalize else z

    def group(g, carry):
        t0 = pl.multiple_of(g * PEER_NSLOT, PEER_NSLOT)
        load_group(t0)
        for i in range(PEER_NSLOT):
            wait_all(i)
            issue(idx_ref, t0 + i + ahead, (i + ahead) % PEER_NSLOT)
            compute(t0 + i, i)
        store_group(t0)
        return carry

    @pl.when(step == 0)
    def _():
        for t in range(ahead):
            issue_rolled(idx_ref, t, t)

    n_groups = tb // PEER_NSLOT
    lax.fori_loop(0, n_groups - 1, group, 0)
    t0 = (n_groups - 1) * PEER_NSLOT
    load_group(t0)
    for i in range(PEER_NSLOT):
        wait_all(i)
        if i + ahead < PEER_NSLOT:
            issue(idx_ref, t0 + i + ahead, (i + ahead) % PEER_NSLOT)
        else:
            @pl.when(step + 1 < pl.num_programs(0))
            def _(i=i):
                issue(idx_next_ref, i + ahead - PEER_NSLOT, (i + ahead) % PEER_NSLOT)
        compute(t0 + i, i)
    store_group(t0)


def peer_apply(experts, h, x1, gate_blocks, uv, final_gain=None):
    T, D = h.shape
    n_blocks, _, tb = gate_blocks.shape
    assert PEER_NSLOT == SUBLANE and tb % PEER_NSLOT == 0
    normalize = final_gain is not None
    gain = (final_gain if normalize else jnp.ones((D,), F32)).reshape(1, D)
    tok_spec = pl.BlockSpec((tb, D), lambda i: (i, 0))
    return pl.pallas_call(
        functools.partial(_peer_apply_kernel, tb=tb, normalize=normalize),
        grid=(n_blocks,),
        in_specs=[
            pl.BlockSpec((tb, PEER_SLOTS), lambda i: (i, 0), memory_space=pltpu.SMEM),
            pl.BlockSpec((tb, PEER_SLOTS), lambda i: (jnp.minimum(i + 1, n_blocks - 1), 0), memory_space=pltpu.SMEM),
            tok_spec,
            tok_spec,
            pl.BlockSpec((1, D), lambda i: (0, 0)),
            pl.BlockSpec((None, PEER_SLOTS, tb), lambda i: (i, 0, 0)),
            pl.BlockSpec(memory_space=pl.ANY),
        ],
        out_specs=tok_spec,
        out_shape=jax.ShapeDtypeStruct((T, D), F32),
        scratch_shapes=[pltpu.VMEM((PEER_SLOTS * UV_ROWS, LANE), BF16) for _ in range(PEER_NSLOT)] + [
            pltpu.VMEM((PEER_SLOTS * SUBLANE, LANE), F32), pltpu.VMEM((PEER_SLOTS, LANE), F32),
            pltpu.VMEM((PEER_NSLOT, D), F32), pltpu.VMEM((PEER_NSLOT, D), F32),
            pltpu.SemaphoreType.DMA((PEER_NSLOT,))],
        compiler_params=_params("arbitrary"),
        name="peer_apply",
    )(experts, experts, h, x1, gain, gate_blocks, uv)


def _pack_w_in(w):
    u, v, q, kv, gate, merge = jnp.split(w, [2048, 4096, 6144, 9216, 9264], axis=1)
    gate = gate.reshape(D_MODEL, NSA_KV_HEADS, NSA_GROUP * 3)
    gate = jnp.pad(gate, ((0, 0), (0, 0), (0, GATE_PAD - NSA_GROUP * 3))).reshape(D_MODEL, NSA_KV_HEADS * GATE_PAD)
    part_a = jnp.concatenate([u, v, q, merge], axis=1).astype(BF16)
    part_b = jnp.concatenate([kv, gate], axis=1).astype(BF16)
    return part_a, part_b


def _rope_tables(S):
    half = HEAD_DIM // 2
    inv = ROPE_THETA ** (-jnp.arange(half, dtype=F32) / half)
    ang = jnp.arange(S, dtype=F32)[:, None] * inv[None, :]
    cos = jnp.cos(ang)
    sin = jnp.sin(ang)
    return jnp.concatenate([cos, cos], axis=1), jnp.concatenate([-sin, sin], axis=1)


def _layer(x, norm_mix_g, w_in, w_out, gm_ln_g, gm_ln_b, gm_spatial_w, gm_spatial_b,
           cmp_pos_k, cmp_w1_k, cmp_w2_k, cmp_pos_v, cmp_w1_v, cmp_w2_v,
           norm_ffn_g, peer_w_q, peer_keys1, peer_keys2, peer_u, peer_v, final_gain):
    B, S, D = x.shape
    T = B * S
    xt = x.reshape(T, D)
    h = rmsnorm_cast(xt, norm_mix_g)
    w_a, w_b = _pack_w_in(w_in)
    proj = matmul(h, w_a, 1024, 2048, BF16, "in_proj_a")
    proj_b = matmul(h, w_b, 1024, COL_B_TOTAL // 2, F32, "in_proj_b")
    o_a = gmlp_mixer(proj, gm_ln_g, gm_ln_b, gm_spatial_w, gm_spatial_b)
    proj3 = proj.reshape(B, S, COL_A_TOTAL)
    proj_b3 = proj_b.reshape(B, S, COL_B_TOTAL)
    cos, sin = _rope_tables(S)
    kc, vc, ks, vs, kw, vw = nsa_prep(proj_b3, cos, sin, cmp_pos_k, cmp_w1_k, cmp_w2_k, cmp_pos_v, cmp_w1_v, cmp_w2_v)
    o_b = nsa_attention(proj3, proj_b3, kc, vc, ks, vs, kw, vw, cos, sin).reshape(T, D)
    x1, h2, h2b = merge_out(proj, o_a, o_b, xt, w_out.astype(BF16), norm_ffn_g)
    qp = matmul(h2b, peer_w_q.astype(BF16), 1024, 1024, F32, "peer_query")
    experts, gate_blocks = peer_route(qp, peer_keys1, peer_keys2, PEER_TOKEN_BLOCK)
    out = peer_apply(experts, h2, x1, gate_blocks, _pack_uv(peer_u, peer_v), final_gain)
    return out.reshape(B, S, D)


def kernel(x, norm_mix_g, w_in, w_out, gm_ln_g, gm_ln_b, gm_spatial_w, gm_spatial_b, cmp_pos_k, cmp_w1_k, cmp_w2_k, cmp_pos_v, cmp_w1_v, cmp_w2_v, norm_ffn_g, peer_w_q, peer_keys1, peer_keys2, peer_u, peer_v, norm_final_g):
    depth = w_in.shape[0]
    for l in range(depth):
        x = _layer(x, norm_mix_g[l], w_in[l], w_out[l], gm_ln_g[l], gm_ln_b[l], gm_spatial_w[l], gm_spatial_b[l],
                   cmp_pos_k[l], cmp_w1_k[l], cmp_w2_k[l], cmp_pos_v[l], cmp_w1_v[l], cmp_w2_v[l],
                   norm_ffn_g[l], peer_w_q[l], peer_keys1[l], peer_keys2[l], peer_u[l], peer_v[l],
                   norm_final_g if l == depth - 1 else None)
    return x
```

```python
import functools
import math

import numpy as np
import jax
import jax.numpy as jnp
from jax import lax
from jax.experimental import pallas as pl
from jax.experimental.pallas import tpu as pltpu

D_MODEL = 2048
GM_GROUPS = 16
GM_GROUP_DIM = 128
GM_CHUNK = 128

NSA_HEADS = 16
NSA_KV_HEADS = 4
NSA_GROUP = 4
HEAD_DIM = 128
CMP_BLOCK = 32
CMP_STRIDE = 16
CMP_HIDDEN = 256
SLC_BLOCK = 64
N_SELECT = 16
WINDOW = 512
ROPE_THETA = 10000.0

PEER_HEADS = 8
PEER_N_KEYS = 128
PEER_TOPK = 16
PEER_SLOTS = PEER_HEADS * PEER_TOPK

EPS = 1e-6
NEG_BIG = -1e30
POS_BIG = 1e30

VMEM_LIMIT_BYTES = 56 * 1024 * 1024
LANE = 128
SUBLANE = 8

COL_U = 0
COL_V = 2048
COL_Q = 4096
COL_MERGE_A = 6144
COL_MERGE_B = 8192
COL_A_TOTAL = 10240
COL_KV = 0
COL_GATE = 3072
GATE_PAD = 128
COL_B_TOTAL = COL_GATE + NSA_KV_HEADS * GATE_PAD

BF16 = jnp.bfloat16
F32 = jnp.float32


def _params(*sem):
    return pltpu.CompilerParams(dimension_semantics=sem, vmem_limit_bytes=VMEM_LIMIT_BYTES)


def _gelu(x):
    return 0.5 * x * (1.0 + jnp.tanh(math.sqrt(2.0 / math.pi) * (x + 0.044715 * (x * x * x))))


def _sigmoid(x):
    return 1.0 / (1.0 + jnp.exp(-x))


def _rms(x, g):
    return x * lax.rsqrt(jnp.mean(x * x, axis=-1, keepdims=True) + EPS) * g


def _dot_nt(a, b, **kw):
    return lax.dot_general(a, b, (((1,), (1,)), ((), ())), preferred_element_type=F32, **kw)


def _rmsnorm_kernel(x_ref, g_ref, o_ref):
    o_ref[...] = _rms(x_ref[...], g_ref[...]).astype(o_ref.dtype)


def rmsnorm_cast(x, g, tm=512):
    T, D = x.shape
    return pl.pallas_call(
        _rmsnorm_kernel,
        grid=(T // tm,),
        in_specs=[pl.BlockSpec((tm, D), lambda i: (i, 0)), pl.BlockSpec((1, D), lambda i: (0, 0))],
        out_specs=pl.BlockSpec((tm, D), lambda i: (i, 0)),
        out_shape=jax.ShapeDtypeStruct((T, D), BF16),
        compiler_params=_params("parallel"),
        name="rmsnorm_cast",
    )(x, g.reshape(1, D))


def _mm_kernel(a_ref, w_ref, o_ref):
    o_ref[...] = jnp.dot(a_ref[...], w_ref[...], preferred_element_type=F32).astype(o_ref.dtype)


def matmul(a, w, tm, tn, out_dtype, name):
    M, K = a.shape
    N = w.shape[1]
    return pl.pallas_call(
        _mm_kernel,
        grid=(N // tn, M // tm),
        in_specs=[pl.BlockSpec((tm, K), lambda j, i: (i, 0)), pl.BlockSpec((K, tn), lambda j, i: (0, j))],
        out_specs=pl.BlockSpec((tm, tn), lambda j, i: (i, j)),
        out_shape=jax.ShapeDtypeStruct((M, N), out_dtype),
        compiler_params=_params("parallel", "parallel"),
        name=name,
    )(a, w)


def _gmlp_kernel(u_ref, v_ref, g_ref, b_ref, w_ref, bs_ref, o_ref):
    gv = _gelu(v_ref[...].astype(F32))
    mu = jnp.mean(gv, axis=-1, keepdims=True)
    xc = gv - mu
    vn = xc * lax.rsqrt(jnp.mean(xc * xc, axis=-1, keepdims=True) + EPS) * g_ref[...] + b_ref[...]
    vn = vn.astype(BF16)
    row = lax.broadcasted_iota(jnp.int32, (GM_CHUNK, GM_CHUNK), 0)
    col = lax.broadcasted_iota(jnp.int32, (GM_CHUNK, GM_CHUNK), 1)
    causal = row >= col
    bs = bs_ref[...]
    for g in range(GM_GROUPS):
        sl = slice(g * GM_GROUP_DIM, (g + 1) * GM_GROUP_DIM)
        w = jnp.where(causal, w_ref[g], 0.0).astype(BF16)
        z = jnp.dot(w, vn[:, sl], preferred_element_type=F32) + bs[:, g:g + 1]
        o_ref[:, sl] = (_gelu(u_ref[:, sl].astype(F32)) * z).astype(o_ref.dtype)


def gmlp_mixer(proj, ln_g, ln_b, w_s, b_s):
    T = proj.shape[0]
    W = GM_GROUPS * GM_GROUP_DIM
    return pl.pallas_call(
        _gmlp_kernel,
        grid=(T // GM_CHUNK,),
        in_specs=[
            pl.BlockSpec((GM_CHUNK, W), lambda i: (i, COL_U // W)),
            pl.BlockSpec((GM_CHUNK, W), lambda i: (i, COL_V // W)),
            pl.BlockSpec((1, W), lambda i: (0, 0)),
            pl.BlockSpec((1, W), lambda i: (0, 0)),
            pl.BlockSpec((GM_GROUPS, GM_CHUNK, GM_CHUNK), lambda i: (0, 0, 0)),
            pl.BlockSpec((GM_CHUNK, GM_GROUPS), lambda i: (0, 0)),
        ],
        out_specs=pl.BlockSpec((GM_CHUNK, W), lambda i: (i, 0)),
        out_shape=jax.ShapeDtypeStruct((T, W), BF16),
        compiler_params=_params("parallel"),
        name="gmlp_mixer",
    )(proj, proj, ln_g.reshape(1, W), ln_b.reshape(1, W), w_s, b_s.T)


def _rope(x, cos, sin_signed):
    return x * cos + pltpu.roll(x, HEAD_DIM // 2, axis=1) * sin_signed


def _compress(k_ref, pos_ref, w1_ref, w2_ref, nc):
    half = CMP_BLOCK // 2
    a = jnp.zeros((nc, CMP_HIDDEN), F32)
    b = jnp.zeros((nc, CMP_HIDDEN), F32)
    for l in range(half):
        kl = k_ref[0, pl.ds(l, nc, stride=CMP_STRIDE), :]
        a = a + jnp.dot((kl + pos_ref[l:l + 1, :]).astype(BF16), w1_ref[l], preferred_element_type=F32)
        b = b + jnp.dot((kl + pos_ref[half + l:half + l + 1, :]).astype(BF16), w1_ref[half + l],
                        preferred_element_type=F32)
    hid = a + pltpu.roll(b, nc - 1, axis=0)
    return jnp.dot(_gelu(hid).astype(BF16), w2_ref[...], preferred_element_type=F32)


def _nsa_prep_kernel(kc_ref, vc_ref, ks_ref, vs_ref, kw_ref, vw_ref, cos_ref, sin_ref,
                     pk_ref, w1k_ref, w2k_ref, pv_ref, w1v_ref, w2v_ref,
                     okc_ref, ovc_ref, oks_ref, ovs_ref, okw_ref, ovw_ref, *, nc):
    okc_ref[0, 0] = _compress(kc_ref, pk_ref, w1k_ref, w2k_ref, nc).astype(okc_ref.dtype)
    ovc_ref[0, 0] = _compress(vc_ref, pv_ref, w1v_ref, w2v_ref, nc).astype(ovc_ref.dtype)
    cos = cos_ref[...]
    sin = sin_ref[...]
    oks_ref[0, 0] = _rope(ks_ref[0], cos, sin).astype(oks_ref.dtype)
    okw_ref[0, 0] = _rope(kw_ref[0], cos, sin).astype(okw_ref.dtype)
    ovs_ref[0, 0] = vs_ref[0].astype(ovs_ref.dtype)
    ovw_ref[0, 0] = vw_ref[0].astype(ovw_ref.dtype)


def nsa_prep(proj3, cos, sin, pos_k, w1_k, w2_k, pos_v, w1_v, w2_v):
    B, S, _ = proj3.shape
    nc = S // CMP_STRIDE
    kvb = COL_KV // HEAD_DIM

    def kv_spec(i):
        return pl.BlockSpec((1, S, HEAD_DIM), lambda b, h, i=i: (b, 0, kvb + i * NSA_KV_HEADS + h))

    full2 = lambda shp: pl.BlockSpec(shp, lambda b, h: (0, 0))
    full3 = lambda shp: pl.BlockSpec(shp, lambda b, h: (0, 0, 0))
    out_c = pl.BlockSpec((1, 1, nc, HEAD_DIM), lambda b, h: (b, h, 0, 0))
    out_s = pl.BlockSpec((1, 1, S, HEAD_DIM), lambda b, h: (b, h, 0, 0))
    shp_c = jax.ShapeDtypeStruct((B, NSA_KV_HEADS, nc, HEAD_DIM), BF16)
    shp_s = jax.ShapeDtypeStruct((B, NSA_KV_HEADS, S, HEAD_DIM), BF16)
    w1k = w1_k.astype(BF16).reshape(CMP_BLOCK, HEAD_DIM, CMP_HIDDEN)
    w1v = w1_v.astype(BF16).reshape(CMP_BLOCK, HEAD_DIM, CMP_HIDDEN)
    return pl.pallas_call(
        functools.partial(_nsa_prep_kernel, nc=nc),
        grid=(B, NSA_KV_HEADS),
        in_specs=[kv_spec(i) for i in range(6)] + [
            full2((S, HEAD_DIM)), full2((S, HEAD_DIM)),
            full2((CMP_BLOCK, HEAD_DIM)), full3((CMP_BLOCK, HEAD_DIM, CMP_HIDDEN)), full2((CMP_HIDDEN, HEAD_DIM)),
            full2((CMP_BLOCK, HEAD_DIM)), full3((CMP_BLOCK, HEAD_DIM, CMP_HIDDEN)), full2((CMP_HIDDEN, HEAD_DIM)),
        ],
        out_specs=[out_c, out_c, out_s, out_s, out_s, out_s],
        out_shape=[shp_c, shp_c, shp_s, shp_s, shp_s, shp_s],
        compiler_params=_params("parallel", "parallel"),
        name="nsa_prep",
    )(proj3, proj3, proj3, proj3, proj3, proj3, cos, sin,
      pos_k, w1k, w2_k.astype(BF16), pos_v, w1v, w2_v.astype(BF16))


def _softmax_rows(s, mask):
    s = jnp.where(mask, s, NEG_BIG)
    m = jnp.max(s, axis=-1, keepdims=True)
    e = jnp.exp(s - m)
    return e / jnp.sum(e, axis=-1, keepdims=True)


def _nsa_attn_kernel(q_ref, gate_ref, kc_ref, vc_ref, ks_ref, vs_ref, kw_ref, vw_ref,
                     cos_ref, sin_ref, agg_ref, exp_ref, o_ref, m_ref, l_ref, acc_ref, *, tq, S, n_sel, wk):
    qt = pl.program_id(2)
    nc = S // CMP_STRIDE
    n_slc = S // SLC_BLOCK
    scale = HEAD_DIM ** -0.5
    pos = qt * tq + lax.broadcasted_iota(jnp.int32, (tq, 1), 0)
    cos = cos_ref[...]
    sin = sin_ref[...]
    gates = _sigmoid(gate_ref[0])

    qs = [q_ref[0, :, g * HEAD_DIM:(g + 1) * HEAD_DIM].astype(F32) for g in range(NSA_GROUP)]
    qr = [(_rope(q, cos, sin) * scale).astype(BF16) for q in qs]

    kc = kc_ref[0, 0]
    vc = vc_ref[0, 0]
    pos_t = qt * tq + lax.broadcasted_iota(jnp.int32, (1, tq), 1)
    cmp_end = lax.broadcasted_iota(jnp.int32, (nc, 1), 0) * CMP_STRIDE + (CMP_BLOCK - 1)
    cmask = (cmp_end <= pos_t) & (cmp_end < S)
    any_c = pos_t >= CMP_BLOCK - 1
    o_cmp = []
    imp = jnp.zeros((nc, tq), F32)
    for g in range(NSA_GROUP):
        s = jnp.where(cmask, _dot_nt(kc, (qs[g] * scale).astype(BF16)), NEG_BIG)
        e = jnp.exp(s - jnp.max(s, axis=0, keepdims=True))
        p = jnp.where(any_c, e / jnp.sum(e, axis=0, keepdims=True), 0.0)
        o_cmp.append(jnp.dot(p.T.astype(BF16), vc, preferred_element_type=F32))
        imp = imp + p

    imp_slc = jnp.dot(agg_ref[...], imp, preferred_element_type=F32, precision=lax.Precision.HIGHEST)
    blk = lax.broadcasted_iota(jnp.int32, (n_slc, tq), 0)
    allowed = blk * SLC_BLOCK <= pos_t
    forced = (blk == 0) | (blk == pos_t // SLC_BLOCK)
    score = jnp.where(forced, POS_BIG, jnp.where(allowed, imp_slc, NEG_BIG))
    rank = jnp.zeros((n_slc, tq), F32)
    for i in range(n_slc):
        si = score[i:i + 1, :]
        tie_before = jnp.where(blk > i, 1.0, 0.0)
        rank = rank + jnp.where(si > score, 1.0, jnp.where(si == score, tie_before, 0.0))
    sel_t = jnp.where(rank < n_sel, 1.0, 0.0)
    pad_rows = LANE - n_slc
    sel = jnp.concatenate([sel_t, jnp.zeros((pad_rows, tq), F32)], axis=0).T[:, :n_slc].astype(BF16)

    m_ref[...] = jnp.full(m_ref.shape, NEG_BIG, F32)
    l_ref[...] = jnp.zeros(l_ref.shape, F32)
    acc_ref[...] = jnp.zeros(acc_ref.shape, F32)

    def slc_chunk(c, carry):
        off = pl.multiple_of(c * tq, tq)
        k_c = ks_ref[0, 0, pl.ds(off, tq), :]
        v_c = vs_ref[0, 0, pl.ds(off, tq), :]
        in_sel = jnp.dot(sel, exp_ref[c], preferred_element_type=F32) > 0.5
        valid = in_sel & (off + lax.broadcasted_iota(jnp.int32, (1, tq), 1) <= pos)
        for g in range(NSA_GROUP):
            s = jnp.where(valid, _dot_nt(qr[g], k_c), NEG_BIG)
            m_old = m_ref[g]
            m_new = jnp.maximum(m_old, jnp.max(s, axis=-1, keepdims=True))
            alpha = jnp.exp(m_old - m_new)
            p = jnp.exp(s - jnp.concatenate([m_new] * (tq // LANE), axis=1))
            l_ref[g] = alpha * l_ref[g] + jnp.sum(p, axis=-1, keepdims=True)
            acc_ref[g] = alpha * acc_ref[g] + jnp.dot(p.astype(BF16), v_c, preferred_element_type=F32)
            m_ref[g] = m_new
        return carry

    lax.fori_loop(0, qt + 1, slc_chunk, 0)
    o_slc = [acc_ref[g] / l_ref[g] for g in range(NSA_GROUP)]

    start = pl.multiple_of(jnp.maximum(qt * tq - (wk - tq), 0), tq)
    kw = kw_ref[0, 0, pl.ds(start, wk), :]
    vw = vw_ref[0, 0, pl.ds(start, wk), :]
    diff = pos - (start + lax.broadcasted_iota(jnp.int32, (1, wk), 1))
    wmask = (diff >= 0) & (diff < WINDOW)
    for g in range(NSA_GROUP):
        s = _dot_nt(qr[g], kw)
        p = _softmax_rows(s, wmask)
        o_win = jnp.dot(p.astype(BF16), vw, preferred_element_type=F32)
        c = g * 3
        o = gates[:, c:c + 1] * o_cmp[g] + gates[:, c + 1:c + 2] * o_slc[g] + gates[:, c + 2:c + 3] * o_win
        o_ref[0, :, g * HEAD_DIM:(g + 1) * HEAD_DIM] = o.astype(o_ref.dtype)


def _selection_constants(S, tq):
    nc = S // CMP_STRIDE
    n_cmp = (S - CMP_BLOCK) // CMP_STRIDE + 1
    n_slc = S // SLC_BLOCK
    ratio = SLC_BLOCK // CMP_STRIDE
    span = CMP_BLOCK // CMP_STRIDE
    agg_w = np.convolve(np.ones(ratio), np.ones(span))
    agg = np.zeros((nc, n_slc), np.float32)
    for j in range(n_slc):
        for w in range(ratio + span - 1):
            c = ratio * j + w - (span - 1)
            if 0 <= c < n_cmp:
                agg[c, j] += agg_w[w]
    expand = (np.arange(S)[None, :] // SLC_BLOCK == np.arange(n_slc)[:, None]).astype(np.float32)
    expand = expand.reshape(n_slc, S // tq, tq).transpose(1, 0, 2)
    return jnp.asarray(agg.T), jnp.asarray(expand, dtype=BF16)


def nsa_attention(proj3, gate3, kc, vc, ks, vs, kw, vw, cos, sin, tq=512):
    B, S, _ = proj3.shape
    nc = S // CMP_STRIDE
    n_slc = S // SLC_BLOCK
    n_sel = min(N_SELECT, n_slc)
    wk = min(S, WINDOW + tq)
    agg, expand = _selection_constants(S, tq)
    qw = NSA_GROUP * HEAD_DIM
    cspec = pl.BlockSpec((1, 1, nc, HEAD_DIM), lambda b, h, t: (b, h, 0, 0))
    sspec = pl.BlockSpec((1, 1, S, HEAD_DIM), lambda b, h, t: (b, h, 0, 0))
    return pl.pallas_call(
        functools.partial(_nsa_attn_kernel, tq=tq, S=S, n_sel=n_sel, wk=wk),
        grid=(B, NSA_KV_HEADS, S // tq),
        in_specs=[
            pl.BlockSpec((1, tq, qw), lambda b, h, t: (b, t, COL_Q // qw + h)),
            pl.BlockSpec((1, tq, GATE_PAD), lambda b, h, t: (b, t, COL_GATE // GATE_PAD + h)),
            cspec, cspec, sspec, sspec, sspec, sspec,
            pl.BlockSpec((tq, HEAD_DIM), lambda b, h, t: (t, 0)),
            pl.BlockSpec((tq, HEAD_DIM), lambda b, h, t: (t, 0)),
            pl.BlockSpec((n_slc, nc), lambda b, h, t: (0, 0)),
            pl.BlockSpec((S // tq, n_slc, tq), lambda b, h, t: (0, 0, 0)),
        ],
        out_specs=pl.BlockSpec((1, tq, qw), lambda b, h, t: (b, t, h)),
        out_shape=jax.ShapeDtypeStruct((B, S, NSA_HEADS * HEAD_DIM), BF16),
        scratch_shapes=[pltpu.VMEM((NSA_GROUP, tq, HEAD_DIM), F32) for _ in range(3)],
        compiler_params=_params("parallel", "parallel", "parallel"),
        name="nsa_attention",
    )(proj3, gate3, kc, vc, ks, vs, kw, vw, cos, sin, agg, expand)


def _merge_out_kernel(ma_ref, mb_ref, oa_ref, ob_ref, x_ref, w_ref, g_ref, x1_ref, h_ref, hb_ref):
    m = (_sigmoid(ma_ref[...].astype(F32)) * oa_ref[...].astype(F32)
         + _sigmoid(mb_ref[...].astype(F32)) * ob_ref[...].astype(F32))
    x1 = x_ref[...] + jnp.dot(m.astype(BF16), w_ref[...], preferred_element_type=F32)
    x1_ref[...] = x1
    h = _rms(x1, g_ref[...])
    h_ref[...] = h
    hb_ref[...] = h.astype(hb_ref.dtype)


def merge_out(proj, o_a, o_b, x, w_out, g, tm=256):
    T, D = x.shape
    row = lambda c: pl.BlockSpec((tm, D), lambda i, c=c: (i, c))
    return pl.pallas_call(
        _merge_out_kernel,
        grid=(T // tm,),
        in_specs=[row(COL_MERGE_A // D), row(COL_MERGE_B // D), row(0), row(0), row(0),
                  pl.BlockSpec((D, D), lambda i: (0, 0)), pl.BlockSpec((1, D), lambda i: (0, 0))],
        out_specs=[row(0), row(0), row(0)],
        out_shape=[jax.ShapeDtypeStruct((T, D), F32), jax.ShapeDtypeStruct((T, D), F32),
                   jax.ShapeDtypeStruct((T, D), BF16)],
        compiler_params=_params("parallel"),
        name="merge_out",
    )(proj, proj, o_a, o_b, x, w_out, g.reshape(1, D))


def _topk_rows(vals, order, payload, k):
    big = float(2 ** 30)
    out_v, out_p = [], []
    for _ in range(k):
        m = jnp.max(vals, axis=0, keepdims=True)
        first = jnp.min(jnp.where(vals == m, order, big), axis=0, keepdims=True)
        hit = order == first
        out_v.append(m)
        out_p.append(first if payload is order else jnp.max(jnp.where(hit, payload, -1.0), axis=0, keepdims=True))
        vals = jnp.where(hit, -jnp.inf, vals)
    return jnp.concatenate(out_v, axis=0), jnp.concatenate(out_p, axis=0)


def _product_key_cells():
    pieces = []
    a = 0
    while a < PEER_TOPK and PEER_TOPK // (a + 1) > 1:
        n_b = PEER_TOPK // (a + 1)
        pieces.append((a, 1, n_b, -(-n_b // SUBLANE) * SUBLANE))
        a += 1
    pieces.append((a, PEER_TOPK - a, 1, 1))
    return pieces


def _peer_route_kernel(q_ref, k1_ref, k2_ref, e_ref, g_ref):
    tt = q_ref.shape[0]
    key_row = lax.broadcasted_iota(jnp.int32, (PEER_N_KEYS, tt), 0).astype(F32)
    pieces = _product_key_cells()
    flat = []
    for a0, n_a, n_b, rows in pieces:
        if n_a == 1:
            flat.append(a0 * PEER_TOPK + lax.broadcasted_iota(jnp.int32, (rows, tt), 0))
        else:
            flat.append((a0 + lax.broadcasted_iota(jnp.int32, (n_a, tt), 0)) * PEER_TOPK)
    flat = jnp.concatenate(flat, axis=0).astype(F32)
    experts, gates = [], []
    for h in range(PEER_HEADS):
        tops = []
        for c, k_ref in enumerate((k1_ref, k2_ref)):
            qh = q_ref[:, (2 * h + c) * PEER_N_KEYS:(2 * h + c + 1) * PEER_N_KEYS]
            s = _dot_nt(k_ref[...], qh, precision=lax.Precision.HIGHEST)
            tops.append(_topk_rows(s, key_row, key_row, PEER_TOPK))
        (v1, i1), (v2, i2) = tops
        cand, cand_id = [], []
        for a0, n_a, n_b, rows in pieces:
            if n_a == 1:
                vals = v1[a0:a0 + 1, :] + v2[:rows, :]
                if n_b < rows:
                    vals = jnp.where(lax.broadcasted_iota(jnp.int32, (rows, tt), 0) < n_b, vals, -jnp.inf)
                cand.append(vals)
                cand_id.append(i1[a0:a0 + 1, :] * PEER_N_KEYS + i2[:rows, :])
            else:
                cand.append(v1[a0:a0 + n_a, :] + v2[0:1, :])
                cand_id.append(i1[a0:a0 + n_a, :] * PEER_N_KEYS + i2[0:1, :])
        top_s, top_e = _topk_rows(jnp.concatenate(cand, axis=0), flat, jnp.concatenate(cand_id, axis=0), PEER_TOPK)
        ex = jnp.exp(top_s - jnp.max(top_s, axis=0, keepdims=True))
        experts.append(top_e)
        gates.append(ex / jnp.sum(ex, axis=0, keepdims=True))
    e_ref[...] = jnp.concatenate(experts, axis=0).T.astype(jnp.int32)
    gate = jnp.concatenate(gates, axis=0)
    tb = g_ref.shape[2]
    for j in range(tt // tb):
        g_ref[j] = gate[:, j * tb:(j + 1) * tb]


def peer_route(qp, keys1, keys2, tb, tt=128):
    T, Q = qp.shape
    kspec = pl.BlockSpec(keys1.shape, lambda i: (0, 0))
    return pl.pallas_call(
        _peer_route_kernel,
        grid=(T // tt,),
        in_specs=[pl.BlockSpec((tt, Q), lambda i: (i, 0)), kspec, kspec],
        out_specs=[pl.BlockSpec((tt, PEER_SLOTS), lambda i: (i, 0)),
                   pl.BlockSpec((tt // tb, PEER_SLOTS, tb), lambda i: (i, 0, 0))],
        out_shape=[jax.ShapeDtypeStruct((T, PEER_SLOTS), jnp.int32),
                   jax.ShapeDtypeStruct((T // tb, PEER_SLOTS, tb), F32)],
        compiler_params=_params("parallel"),
        name="peer_route",
    )(qp, keys1, keys2)


X_ROWS = D_MODEL // LANE
UV_ROWS = 2 * X_ROWS
PEER_NSLOT = 8
PEER_TOKEN_BLOCK = 64
DMA_PRIORITIES = 2


def _pack_uv(u, v):
    n_exp = u.shape[0]
    return jnp.concatenate([u.reshape(n_exp, X_ROWS, LANE), v.reshape(n_exp, X_ROWS, LANE)], axis=1).astype(BF16)


def _peer_apply_kernel(idx_ref, idx_next_ref, h_ref, x1_ref, gn_ref, gt_ref, uv_hbm, o_ref, *scratch, tb, normalize):
    bufs = scratch[:PEER_NSLOT]
    p_ref, w_ref, xg_ref, og_ref, sem = scratch[PEER_NSLOT:]
    ahead = PEER_NSLOT - 1
    step = pl.program_id(0)

    def row_copy(ids, t, j, slot, row):
        return pltpu.make_async_copy(uv_hbm.at[ids[t, j]], bufs[slot].at[pl.ds(row, UV_ROWS), :], sem.at[slot])

    def issue(ids, t, slot):
        for j in range(PEER_SLOTS):
            row_copy(ids, t, j, slot, j * UV_ROWS).start(priority=j % DMA_PRIORITIES)

    def issue_rolled(ids, t, slot):
        def body(j, carry):
            row_copy(ids, t, j, slot, pl.multiple_of(j * UV_ROWS, UV_ROWS)).start()
            return carry
        lax.fori_loop(0, PEER_SLOTS, body, 0)

    def wait_all(slot):
        pltpu.make_async_copy(bufs[(slot + 1) % PEER_NSLOT], bufs[slot], sem.at[slot]).wait()

    lane = lax.broadcasted_iota(jnp.int32, (PEER_SLOTS, tb), 1)

    def compute(t, slot):
        buf = bufs[slot]
        x = jnp.concatenate([xg_ref[slot:slot + 1, c * LANE:(c + 1) * LANE] for c in range(X_ROWS)], axis=0)
        for j in range(PEER_SLOTS):
            prod = buf[j * UV_ROWS:j * UV_ROWS + X_ROWS, :].astype(F32) * x
            p_ref[j * SUBLANE:(j + 1) * SUBLANE, :] = prod[:SUBLANE] + prod[SUBLANE:]
        acc = p_ref[pl.ds(0, PEER_SLOTS, stride=SUBLANE), :]
        for q in range(1, SUBLANE):
            acc = acc + p_ref[pl.ds(q, PEER_SLOTS, stride=SUBLANE), :]
        act = _gelu(jnp.sum(acc, axis=1, keepdims=True))
        gate = jnp.sum(jnp.where(lane == t, gt_ref[...], 0.0), axis=1, keepdims=True)
        w_ref[...] = jnp.broadcast_to(gate * act, (PEER_SLOTS, LANE))
        n_acc = 4
        out = [jnp.zeros((X_ROWS, LANE), F32) for _ in range(n_acc)]
        for j in range(PEER_SLOTS):
            base = j * UV_ROWS + X_ROWS
            out[j % n_acc] = out[j % n_acc] + buf[base:base + X_ROWS, :].astype(F32) * w_ref[j:j + 1, :]
        y = (out[0] + out[1]) + (out[2] + out[3])
        for c in range(X_ROWS):
            og_ref[slot:slot + 1, c * LANE:(c + 1) * LANE] = y[c:c + 1, :]

    def load_group(t0):
        xg_ref[...] = h_ref[pl.ds(t0, PEER_NSLOT), :]

    def store_group(t0):
        z = x1_ref[pl.ds(t0, PEER_NSLOT), :] + og_ref[...]
        o_ref[pl.ds(t0, PEER_NSLOT), :] = _rms(z, gn_ref[...]) if normalize else z

    def group(g, carry):
        t0 = pl.multiple_of(g * PEER_NSLOT, PEER_NSLOT)
        load_group(t0)
        for i in range(PEER_NSLOT):
            wait_all(i)
            issue(idx_ref, t0 + i + ahead, (i + ahead) % PEER_NSLOT)
            compute(t0 + i, i)
        store_group(t0)
        return carry

    @pl.when(step == 0)
    def _():
        for t in range(ahead):
            issue_rolled(idx_ref, t, t)

    n_groups = tb // PEER_NSLOT
    lax.fori_loop(0, n_groups - 1, group, 0)
    t0 = (n_groups - 1) * PEER_NSLOT
    load_group(t0)
    for i in range(PEER_NSLOT):
        wait_all(i)
        if i + ahead < PEER_NSLOT:
            issue(idx_ref, t0 + i + ahead, (i + ahead) % PEER_NSLOT)
        else:
            @pl.when(step + 1 < pl.num_programs(0))
            def _(i=i):
                issue(idx_next_ref, i + ahead - PEER_NSLOT, (i + ahead) % PEER_NSLOT)
        compute(t0 + i, i)
    store_group(t0)


def peer_apply(experts, h, x1, gate_blocks, uv, final_gain=None):
    T, D = h.shape
    n_blocks, _, tb = gate_blocks.shape
    assert PEER_NSLOT == SUBLANE and tb % PEER_NSLOT == 0
    normalize = final_gain is not None
    gain = (final_gain if normalize else jnp.ones((D,), F32)).reshape(1, D)
    tok_spec = pl.BlockSpec((tb, D), lambda i: (i, 0))
    return pl.pallas_call(
        functools.partial(_peer_apply_kernel, tb=tb, normalize=normalize),
        grid=(n_blocks,),
        in_specs=[
            pl.BlockSpec((tb, PEER_SLOTS), lambda i: (i, 0), memory_space=pltpu.SMEM),
            pl.BlockSpec((tb, PEER_SLOTS), lambda i: (jnp.minimum(i + 1, n_blocks - 1), 0), memory_space=pltpu.SMEM),
            tok_spec,
            tok_spec,
            pl.BlockSpec((1, D), lambda i: (0, 0)),
            pl.BlockSpec((None, PEER_SLOTS, tb), lambda i: (i, 0, 0)),
            pl.BlockSpec(memory_space=pl.ANY),
        ],
        out_specs=tok_spec,
        out_shape=jax.ShapeDtypeStruct((T, D), F32),
        scratch_shapes=[pltpu.VMEM((PEER_SLOTS * UV_ROWS, LANE), BF16) for _ in range(PEER_NSLOT)] + [
            pltpu.VMEM((PEER_SLOTS * SUBLANE, LANE), F32), pltpu.VMEM((PEER_SLOTS, LANE), F32),
            pltpu.VMEM((PEER_NSLOT, D), F32), pltpu.VMEM((PEER_NSLOT, D), F32),
            pltpu.SemaphoreType.DMA((PEER_NSLOT,))],
        compiler_params=_params("arbitrary"),
        name="peer_apply",
    )(experts, experts, h, x1, gain, gate_blocks, uv)


def _pack_w_in(w):
    u, v, q, kv, gate, merge = jnp.split(w, [2048, 4096, 6144, 9216, 9264], axis=1)
    gate = gate.reshape(D_MODEL, NSA_KV_HEADS, NSA_GROUP * 3)
    gate = jnp.pad(gate, ((0, 0), (0, 0), (0, GATE_PAD - NSA_GROUP * 3))).reshape(D_MODEL, NSA_KV_HEADS * GATE_PAD)
    part_a = jnp.concatenate([u, v, q, merge], axis=1).astype(BF16)
    part_b = jnp.concatenate([kv, gate], axis=1).astype(BF16)
    return part_a, part_b


def _rope_tables(S):
    half = HEAD_DIM // 2
    inv = ROPE_THETA ** (-jnp.arange(half, dtype=F32) / half)
    ang = jnp.arange(S, dtype=F32)[:, None] * inv[None, :]
    cos = jnp.cos(ang)
    sin = jnp.sin(ang)
    return jnp.concatenate([cos, cos], axis=1), jnp.concatenate([-sin, sin], axis=1)


def _layer(x, norm_mix_g, w_in, w_out, gm_ln_g, gm_ln_b, gm_spatial_w, gm_spatial_b,
           cmp_pos_k, cmp_w1_k, cmp_w2_k, cmp_pos_v, cmp_w1_v, cmp_w2_v,
           norm_ffn_g, peer_w_q, peer_keys1, peer_keys2, peer_u, peer_v, final_gain):
    B, S, D = x.shape
    T = B * S
    xt = x.reshape(T, D)
    h = rmsnorm_cast(xt, norm_mix_g)
    w_a, w_b = _pack_w_in(w_in)
    proj = matmul(h, w_a, 1024, 2048, BF16, "in_proj_a")
    proj_b = matmul(h, w_b, 1024, COL_B_TOTAL // 2, F32, "in_proj_b")
    o_a = gmlp_mixer(proj, gm_ln_g, gm_ln_b, gm_spatial_w, gm_spatial_b)
    proj3 = proj.reshape(B, S, COL_A_TOTAL)
    proj_b3 = proj_b.reshape(B, S, COL_B_TOTAL)
    cos, sin = _rope_tables(S)
    kc, vc, ks, vs, kw, vw = nsa_prep(proj_b3, cos, sin, cmp_pos_k, cmp_w1_k, cmp_w2_k, cmp_pos_v, cmp_w1_v, cmp_w2_v)
    o_b = nsa_attention(proj3, proj_b3, kc, vc, ks, vs, kw, vw, cos, sin).reshape(T, D)
    x1, h2, h2b = merge_out(proj, o_a, o_b, xt, w_out.astype(BF16), norm_ffn_g)
    qp = matmul(h2b, peer_w_q.astype(BF16), 1024, 1024, F32, "peer_query")
    experts, gate_blocks = peer_route(qp, peer_keys1, peer_keys2, PEER_TOKEN_BLOCK)
    out = peer_apply(experts, h2, x1, gate_blocks, _pack_uv(peer_u, peer_v), final_gain)
    return out.reshape(B, S, D)


def kernel(x, norm_mix_g, w_in, w_out, gm_ln_g, gm_ln_b, gm_spatial_w, gm_spatial_b, cmp_pos_k, cmp_w1_k, cmp_w2_k, cmp_pos_v, cmp_w1_v, cmp_w2_v, norm_ffn_g, peer_w_q, peer_keys1, peer_keys2, peer_u, peer_v, norm_final_g):
    depth = w_in.shape[0]
    for l in range(depth):
        x = _layer(x, norm_mix_g[l], w_in[l], w_out[l], gm_ln_g[l], gm_ln_b[l], gm_spatial_w[l], gm_spatial_b[l],
                   cmp_pos_k[l], cmp_w1_k[l], cmp_w2_k[l], cmp_pos_v[l], cmp_w1_v[l], cmp_w2_v[l],
                   norm_ffn_g[l], peer_w_q[l], peer_keys1[l], peer_keys2[l], peer_u[l], peer_v[l],
                   norm_final_g if l == depth - 1 else None)
    return x
```

```python
import functools
import math

import numpy as np
import jax
import jax.numpy as jnp
from jax import lax
from jax.experimental import pallas as pl
from jax.experimental.pallas import tpu as pltpu

D_MODEL = 2048
GM_GROUPS = 16
GM_GROUP_DIM = 128
GM_CHUNK = 128

NSA_HEADS = 16
NSA_KV_HEADS = 4
NSA_GROUP = 4
HEAD_DIM = 128
CMP_BLOCK = 32
CMP_STRIDE = 16
CMP_HIDDEN = 256
SLC_BLOCK = 64
N_SELECT = 16
WINDOW = 512
ROPE_THETA = 10000.0

PEER_HEADS = 8
PEER_N_KEYS = 128
PEER_TOPK = 16
PEER_SLOTS = PEER_HEADS * PEER_TOPK

EPS = 1e-6
NEG_BIG = -1e30
POS_BIG = 1e30

VMEM_LIMIT_BYTES = 56 * 1024 * 1024
LANE = 128
SUBLANE = 8

COL_U = 0
COL_V = 2048
COL_Q = 4096
COL_MERGE_A = 6144
COL_MERGE_B = 8192
COL_A_TOTAL = 10240
COL_KV = 0
COL_GATE = 3072
GATE_PAD = 128
COL_B_TOTAL = COL_GATE + NSA_KV_HEADS * GATE_PAD

BF16 = jnp.bfloat16
F32 = jnp.float32


def _params(*sem):
    return pltpu.CompilerParams(dimension_semantics=sem, vmem_limit_bytes=VMEM_LIMIT_BYTES)


def _gelu(x):
    return 0.5 * x * (1.0 + jnp.tanh(math.sqrt(2.0 / math.pi) * (x + 0.044715 * (x * x * x))))


def _sigmoid(x):
    return 1.0 / (1.0 + jnp.exp(-x))


def _rms(x, g):
    return x * lax.rsqrt(jnp.mean(x * x, axis=-1, keepdims=True) + EPS) * g


def _dot_nt(a, b, **kw):
    return lax.dot_general(a, b, (((1,), (1,)), ((), ())), preferred_element_type=F32, **kw)


def _rmsnorm_kernel(x_ref, g_ref, o_ref):
    o_ref[...] = _rms(x_ref[...], g_ref[...]).astype(o_ref.dtype)


def rmsnorm_cast(x, g, tm=512):
    T, D = x.shape
    return pl.pallas_call(
        _rmsnorm_kernel,
        grid=(T // tm,),
        in_specs=[pl.BlockSpec((tm, D), lambda i: (i, 0)), pl.BlockSpec((1, D), lambda i: (0, 0))],
        out_specs=pl.BlockSpec((tm, D), lambda i: (i, 0)),
        out_shape=jax.ShapeDtypeStruct((T, D), BF16),
        compiler_params=_params("parallel"),
        name="rmsnorm_cast",
    )(x, g.reshape(1, D))


def _mm_kernel(a_ref, w_ref, o_ref):
    o_ref[...] = jnp.dot(a_ref[...], w_ref[...], preferred_element_type=F32).astype(o_ref.dtype)


def matmul(a, w, tm, tn, out_dtype, name):
    M, K = a.shape
    N = w.shape[1]
    return pl.pallas_call(
        _mm_kernel,
        grid=(N // tn, M // tm),
        in_specs=[pl.BlockSpec((tm, K), lambda j, i: (i, 0)), pl.BlockSpec((K, tn), lambda j, i: (0, j))],
        out_specs=pl.BlockSpec((tm, tn), lambda j, i: (i, j)),
        out_shape=jax.ShapeDtypeStruct((M, N), out_dtype),
        compiler_params=_params("parallel", "parallel"),
        name=name,
    )(a, w)


def _gmlp_kernel(u_ref, v_ref, g_ref, b_ref, w_ref, bs_ref, o_ref):
    gv = _gelu(v_ref[...].astype(F32))
    mu = jnp.mean(gv, axis=-1, keepdims=True)
    xc = gv - mu
    vn = xc * lax.rsqrt(jnp.mean(xc * xc, axis=-1, keepdims=True) + EPS) * g_ref[...] + b_ref[...]
    vn = vn.astype(BF16)
    row = lax.broadcasted_iota(jnp.int32, (GM_CHUNK, GM_CHUNK), 0)
    col = lax.broadcasted_iota(jnp.int32, (GM_CHUNK, GM_CHUNK), 1)
    causal = row >= col
    bs = bs_ref[...]
    for g in range(GM_GROUPS):
        sl = slice(g * GM_GROUP_DIM, (g + 1) * GM_GROUP_DIM)
        w = jnp.where(causal, w_ref[g], 0.0).astype(BF16)
        z = jnp.dot(w, vn[:, sl], preferred_element_type=F32) + bs[:, g:g + 1]
        o_ref[:, sl] = (_gelu(u_ref[:, sl].astype(F32)) * z).astype(o_ref.dtype)


def gmlp_mixer(proj, ln_g, ln_b, w_s, b_s):
    T = proj.shape[0]
    W = GM_GROUPS * GM_GROUP_DIM
    return pl.pallas_call(
        _gmlp_kernel,
        grid=(T // GM_CHUNK,),
        in_specs=[
            pl.BlockSpec((GM_CHUNK, W), lambda i: (i, COL_U // W)),
            pl.BlockSpec((GM_CHUNK, W), lambda i: (i, COL_V // W)),
            pl.BlockSpec((1, W), lambda i: (0, 0)),
            pl.BlockSpec((1, W), lambda i: (0, 0)),
            pl.BlockSpec((GM_GROUPS, GM_CHUNK, GM_CHUNK), lambda i: (0, 0, 0)),
            pl.BlockSpec((GM_CHUNK, GM_GROUPS), lambda i: (0, 0)),
        ],
        out_specs=pl.BlockSpec((GM_CHUNK, W), lambda i: (i, 0)),
        out_shape=jax.ShapeDtypeStruct((T, W), BF16),
        compiler_params=_params("parallel"),
        name="gmlp_mixer",
    )(proj, proj, ln_g.reshape(1, W), ln_b.reshape(1, W), w_s, b_s.T)


def _rope(x, cos, sin_signed):
    return x * cos + pltpu.roll(x, HEAD_DIM // 2, axis=1) * sin_signed


def _compress(k_ref, pos_ref, w1_ref, w2_ref, nc):
    half = CMP_BLOCK // 2
    a = jnp.zeros((nc, CMP_HIDDEN), F32)
    b = jnp.zeros((nc, CMP_HIDDEN), F32)
    for l in range(half):
        kl = k_ref[0, pl.ds(l, nc, stride=CMP_STRIDE), :]
        a = a + jnp.dot((kl + pos_ref[l:l + 1, :]).astype(BF16), w1_ref[l], preferred_element_type=F32)
        b = b + jnp.dot((kl + pos_ref[half + l:half + l + 1, :]).astype(BF16), w1_ref[half + l],
                        preferred_element_type=F32)
    hid = a + pltpu.roll(b, nc - 1, axis=0)
    return jnp.dot(_gelu(hid).astype(BF16), w2_ref[...], preferred_element_type=F32)


def _nsa_prep_kernel(kc_ref, vc_ref, ks_ref, vs_ref, kw_ref, vw_ref, cos_ref, sin_ref,
                     pk_ref, w1k_ref, w2k_ref, pv_ref, w1v_ref, w2v_ref,
                     okc_ref, ovc_ref, oks_ref, ovs_ref, okw_ref, ovw_ref, *, nc):
    okc_ref[0, 0] = _compress(kc_ref, pk_ref, w1k_ref, w2k_ref, nc).astype(okc_ref.dtype)
    ovc_ref[0, 0] = _compress(vc_ref, pv_ref, w1v_ref, w2v_ref, nc).astype(ovc_ref.dtype)
    cos = cos_ref[...]
    sin = sin_ref[...]
    oks_ref[0, 0] = _rope(ks_ref[0], cos, sin).astype(oks_ref.dtype)
    okw_ref[0, 0] = _rope(kw_ref[0], cos, sin).astype(okw_ref.dtype)
    ovs_ref[0, 0] = vs_ref[0].astype(ovs_ref.dtype)
    ovw_ref[0, 0] = vw_ref[0].astype(ovw_ref.dtype)


def nsa_prep(proj3, cos, sin, pos_k, w1_k, w2_k, pos_v, w1_v, w2_v):
    B, S, _ = proj3.shape
    nc = S // CMP_STRIDE
    kvb = COL_KV // HEAD_DIM

    def kv_spec(i):
        return pl.BlockSpec((1, S, HEAD_DIM), lambda b, h, i=i: (b, 0, kvb + i * NSA_KV_HEADS + h))

    full2 = lambda shp: pl.BlockSpec(shp, lambda b, h: (0, 0))
    full3 = lambda shp: pl.BlockSpec(shp, lambda b, h: (0, 0, 0))
    out_c = pl.BlockSpec((1, 1, nc, HEAD_DIM), lambda b, h: (b, h, 0, 0))
    out_s = pl.BlockSpec((1, 1, S, HEAD_DIM), lambda b, h: (b, h, 0, 0))
    shp_c = jax.ShapeDtypeStruct((B, NSA_KV_HEADS, nc, HEAD_DIM), BF16)
    shp_s = jax.ShapeDtypeStruct((B, NSA_KV_HEADS, S, HEAD_DIM), BF16)
    w1k = w1_k.astype(BF16).reshape(CMP_BLOCK, HEAD_DIM, CMP_HIDDEN)
    w1v = w1_v.astype(BF16).reshape(CMP_BLOCK, HEAD_DIM, CMP_HIDDEN)
    return pl.pallas_call(
        functools.partial(_nsa_prep_kernel, nc=nc),
        grid=(B, NSA_KV_HEADS),
        in_specs=[kv_spec(i) for i in range(6)] + [
            full2((S, HEAD_DIM)), full2((S, HEAD_DIM)),
            full2((CMP_BLOCK, HEAD_DIM)), full3((CMP_BLOCK, HEAD_DIM, CMP_HIDDEN)), full2((CMP_HIDDEN, HEAD_DIM)),
            full2((CMP_BLOCK, HEAD_DIM)), full3((CMP_BLOCK, HEAD_DIM, CMP_HIDDEN)), full2((CMP_HIDDEN, HEAD_DIM)),
        ],
        out_specs=[out_c, out_c, out_s, out_s, out_s, out_s],
        out_shape=[shp_c, shp_c, shp_s, shp_s, shp_s, shp_s],
        compiler_params=_params("parallel", "parallel"),
        name="nsa_prep",
    )(proj3, proj3, proj3, proj3, proj3, proj3, cos, sin,
      pos_k, w1k, w2_k.astype(BF16), pos_v, w1v, w2_v.astype(BF16))


def _softmax_rows(s, mask):
    s = jnp.where(mask, s, NEG_BIG)
    m = jnp.max(s, axis=-1, keepdims=True)
    e = jnp.exp(s - m)
    return e / jnp.sum(e, axis=-1, keepdims=True)


def _nsa_attn_kernel(q_ref, gate_ref, kc_ref, vc_ref, ks_ref, vs_ref, kw_ref, vw_ref,
                     cos_ref, sin_ref, agg_ref, exp_ref, o_ref, m_ref, l_ref, acc_ref, *, tq, S, n_sel, wk):
    qt = pl.program_id(2)
    nc = S // CMP_STRIDE
    n_slc = S // SLC_BLOCK
    scale = HEAD_DIM ** -0.5
    pos = qt * tq + lax.broadcasted_iota(jnp.int32, (tq, 1), 0)
    cos = cos_ref[...]
    sin = sin_ref[...]
    gates = _sigmoid(gate_ref[0])

    qs = [q_ref[0, :, g * HEAD_DIM:(g + 1) * HEAD_DIM].astype(F32) for g in range(NSA_GROUP)]
    qr = [(_rope(q, cos, sin) * scale).astype(BF16) for q in qs]

    kc = kc_ref[0, 0]
    vc = vc_ref[0, 0]
    pos_t = qt * tq + lax.broadcasted_iota(jnp.int32, (1, tq), 1)
    cmp_end = lax.broadcasted_iota(jnp.int32, (nc, 1), 0) * CMP_STRIDE + (CMP_BLOCK - 1)
    cmask = (cmp_end <= pos_t) & (cmp_end < S)
    any_c = pos_t >= CMP_BLOCK - 1
    o_cmp = []
    imp = jnp.zeros((nc, tq), F32)
    for g in range(NSA_GROUP):
        s = jnp.where(cmask, _dot_nt(kc, (qs[g] * scale).astype(BF16)), NEG_BIG)
        e = jnp.exp(s - jnp.max(s, axis=0, keepdims=True))
        p = jnp.where(any_c, e / jnp.sum(e, axis=0, keepdims=True), 0.0)
        o_cmp.append(jnp.dot(p.T.astype(BF16), vc, preferred_element_type=F32))
        imp = imp + p

    imp_slc = jnp.dot(agg_ref[...], imp, preferred_element_type=F32, precision=lax.Precision.HIGHEST)
    blk = lax.broadcasted_iota(jnp.int32, (n_slc, tq), 0)
    allowed = blk * SLC_BLOCK <= pos_t
    forced = (blk == 0) | (blk == pos_t // SLC_BLOCK)
    score = jnp.where(forced, POS_BIG, jnp.where(allowed, imp_slc, NEG_BIG))
    rank = jnp.zeros((n_slc, tq), F32)
    for i in range(n_slc):
        si = score[i:i + 1, :]
        tie_before = jnp.where(blk > i, 1.0, 0.0)
        rank = rank + jnp.where(si > score, 1.0, jnp.where(si == score, tie_before, 0.0))
    sel_t = jnp.where(rank < n_sel, 1.0, 0.0)
    pad_rows = LANE - n_slc
    sel = jnp.concatenate([sel_t, jnp.zeros((pad_rows, tq), F32)], axis=0).T[:, :n_slc].astype(BF16)

    m_ref[...] = jnp.full(m_ref.shape, NEG_BIG, F32)
    l_ref[...] = jnp.zeros(l_ref.shape, F32)
    acc_ref[...] = jnp.zeros(acc_ref.shape, F32)

    def slc_chunk(c, carry):
        off = pl.multiple_of(c * tq, tq)
        k_c = ks_ref[0, 0, pl.ds(off, tq), :]
        v_c = vs_ref[0, 0, pl.ds(off, tq), :]
        in_sel = jnp.dot(sel, exp_ref[c], preferred_element_type=F32) > 0.5
        valid = in_sel & (off + lax.broadcasted_iota(jnp.int32, (1, tq), 1) <= pos)
        for g in range(NSA_GROUP):
            s = jnp.where(valid, _dot_nt(qr[g], k_c), NEG_BIG)
            m_old = m_ref[g]
            m_new = jnp.maximum(m_old, jnp.max(s, axis=-1, keepdims=True))
            alpha = jnp.exp(m_old - m_new)
            p = jnp.exp(s - jnp.concatenate([m_new] * (tq // LANE), axis=1))
            l_ref[g] = alpha * l_ref[g] + jnp.sum(p, axis=-1, keepdims=True)
            acc_ref[g] = alpha * acc_ref[g] + jnp.dot(p.astype(BF16), v_c, preferred_element_type=F32)
            m_ref[g] = m_new
        return carry

    lax.fori_loop(0, qt + 1, slc_chunk, 0)
    o_slc = [acc_ref[g] / l_ref[g] for g in range(NSA_GROUP)]

    start = pl.multiple_of(jnp.maximum(qt * tq - (wk - tq), 0), tq)
    kw = kw_ref[0, 0, pl.ds(start, wk), :]
    vw = vw_ref[0, 0, pl.ds(start, wk), :]
    diff = pos - (start + lax.broadcasted_iota(jnp.int32, (1, wk), 1))
    wmask = (diff >= 0) & (diff < WINDOW)
    for g in range(NSA_GROUP):
        s = _dot_nt(qr[g], kw)
        p = _softmax_rows(s, wmask)
        o_win = jnp.dot(p.astype(BF16), vw, preferred_element_type=F32)
        c = g * 3
        o = gates[:, c:c + 1] * o_cmp[g] + gates[:, c + 1:c + 2] * o_slc[g] + gates[:, c + 2:c + 3] * o_win
        o_ref[0, :, g * HEAD_DIM:(g + 1) * HEAD_DIM] = o.astype(o_ref.dtype)


def _selection_constants(S, tq):
    nc = S // CMP_STRIDE
    n_cmp = (S - CMP_BLOCK) // CMP_STRIDE + 1
    n_slc = S // SLC_BLOCK
    ratio = SLC_BLOCK // CMP_STRIDE
    span = CMP_BLOCK // CMP_STRIDE
    agg_w = np.convolve(np.ones(ratio), np.ones(span))
    agg = np.zeros((nc, n_slc), np.float32)
    for j in range(n_slc):
        for w in range(ratio + span - 1):
            c = ratio * j + w - (span - 1)
            if 0 <= c < n_cmp:
                agg[c, j] += agg_w[w]
    expand = (np.arange(S)[None, :] // SLC_BLOCK == np.arange(n_slc)[:, None]).astype(np.float32)
    expand = expand.reshape(n_slc, S // tq, tq).transpose(1, 0, 2)
    return jnp.asarray(agg.T), jnp.asarray(expand, dtype=BF16)


def nsa_attention(proj3, gate3, kc, vc, ks, vs, kw, vw, cos, sin, tq=512):
    B, S, _ = proj3.shape
    nc = S // CMP_STRIDE
    n_slc = S // SLC_BLOCK
    n_sel = min(N_SELECT, n_slc)
    wk = min(S, WINDOW + tq)
    agg, expand = _selection_constants(S, tq)
    qw = NSA_GROUP * HEAD_DIM
    cspec = pl.BlockSpec((1, 1, nc, HEAD_DIM), lambda b, h, t: (b, h, 0, 0))
    sspec = pl.BlockSpec((1, 1, S, HEAD_DIM), lambda b, h, t: (b, h, 0, 0))
    return pl.pallas_call(
        functools.partial(_nsa_attn_kernel, tq=tq, S=S, n_sel=n_sel, wk=wk),
        grid=(B, NSA_KV_HEADS, S // tq),
        in_specs=[
            pl.BlockSpec((1, tq, qw), lambda b, h, t: (b, t, COL_Q // qw + h)),
            pl.BlockSpec((1, tq, GATE_PAD), lambda b, h, t: (b, t, COL_GATE // GATE_PAD + h)),
            cspec, cspec, sspec, sspec, sspec, sspec,
            pl.BlockSpec((tq, HEAD_DIM), lambda b, h, t: (t, 0)),
            pl.BlockSpec((tq, HEAD_DIM), lambda b, h, t: (t, 0)),
            pl.BlockSpec((n_slc, nc), lambda b, h, t: (0, 0)),
            pl.BlockSpec((S // tq, n_slc, tq), lambda b, h, t: (0, 0, 0)),
        ],
        out_specs=pl.BlockSpec((1, tq, qw), lambda b, h, t: (b, t, h)),
        out_shape=jax.ShapeDtypeStruct((B, S, NSA_HEADS * HEAD_DIM), BF16),
        scratch_shapes=[pltpu.VMEM((NSA_GROUP, tq, HEAD_DIM), F32) for _ in range(3)],
        compiler_params=_params("parallel", "parallel", "parallel"),
        name="nsa_attention",
    )(proj3, gate3, kc, vc, ks, vs, kw, vw, cos, sin, agg, expand)


def _merge_out_kernel(ma_ref, mb_ref, oa_ref, ob_ref, x_ref, w_ref, g_ref, x1_ref, h_ref, hb_ref):
    m = (_sigmoid(ma_ref[...].astype(F32)) * oa_ref[...].astype(F32)
         + _sigmoid(mb_ref[...].astype(F32)) * ob_ref[...].astype(F32))
    x1 = x_ref[...] + jnp.dot(m.astype(BF16), w_ref[...], preferred_element_type=F32)
    x1_ref[...] = x1
    h = _rms(x1, g_ref[...])
    h_ref[...] = h
    hb_ref[...] = h.astype(hb_ref.dtype)


def merge_out(proj, o_a, o_b, x, w_out, g, tm=256):
    T, D = x.shape
    row = lambda c: pl.BlockSpec((tm, D), lambda i, c=c: (i, c))
    return pl.pallas_call(
        _merge_out_kernel,
        grid=(T // tm,),
        in_specs=[row(COL_MERGE_A // D), row(COL_MERGE_B // D), row(0), row(0), row(0),
                  pl.BlockSpec((D, D), lambda i: (0, 0)), pl.BlockSpec((1, D), lambda i: (0, 0))],
        out_specs=[row(0), row(0), row(0)],
        out_shape=[jax.ShapeDtypeStruct((T, D), F32), jax.ShapeDtypeStruct((T, D), F32),
                   jax.ShapeDtypeStruct((T, D), BF16)],
        compiler_params=_params("parallel"),
        name="merge_out",
    )(proj, proj, o_a, o_b, x, w_out, g.reshape(1, D))


def _topk_rows(vals, order, payload, k):
    big = float(2 ** 30)
    out_v, out_p = [], []
    for _ in range(k):
        m = jnp.max(vals, axis=0, keepdims=True)
        first = jnp.min(jnp.where(vals == m, order, big), axis=0, keepdims=True)
        hit = order == first
        out_v.append(m)
        out_p.append(first if payload is order else jnp.max(jnp.where(hit, payload, -1.0), axis=0, keepdims=True))
        vals = jnp.where(hit, -jnp.inf, vals)
    return jnp.concatenate(out_v, axis=0), jnp.concatenate(out_p, axis=0)


def _product_key_cells():
    pieces = []
    a = 0
    while a < PEER_TOPK and PEER_TOPK // (a + 1) > 1:
        n_b = PEER_TOPK // (a + 1)
        pieces.append((a, 1, n_b, -(-n_b // SUBLANE) * SUBLANE))
        a += 1
    pieces.append((a, PEER_TOPK - a, 1, 1))
    return pieces


def _peer_route_kernel(q_ref, k1_ref, k2_ref, e_ref, g_ref):
    tt = q_ref.shape[0]
    key_row = lax.broadcasted_iota(jnp.int32, (PEER_N_KEYS, tt), 0).astype(F32)
    pieces = _product_key_cells()
    flat = []
    for a0, n_a, n_b, rows in pieces:
        if n_a == 1:
            flat.append(a0 * PEER_TOPK + lax.broadcasted_iota(jnp.int32, (rows, tt), 0))
        else:
            flat.append((a0 + lax.broadcasted_iota(jnp.int32, (n_a, tt), 0)) * PEER_TOPK)
    flat = jnp.concatenate(flat, axis=0).astype(F32)
    experts, gates = [], []
    for h in range(PEER_HEADS):
        tops = []
        for c, k_ref in enumerate((k1_ref, k2_ref)):
            qh = q_ref[:, (2 * h + c) * PEER_N_KEYS:(2 * h + c + 1) * PEER_N_KEYS]
            s = _dot_nt(k_ref[...], qh, precision=lax.Precision.HIGHEST)
            tops.append(_topk_rows(s, key_row, key_row, PEER_TOPK))
        (v1, i1), (v2, i2) = tops
        cand, cand_id = [], []
        for a0, n_a, n_b, rows in pieces:
            if n_a == 1:
                vals = v1[a0:a0 + 1, :] + v2[:rows, :]
                if n_b < rows:
                    vals = jnp.where(lax.broadcasted_iota(jnp.int32, (rows, tt), 0) < n_b, vals, -jnp.inf)
                cand.append(vals)
                cand_id.append(i1[a0:a0 + 1, :] * PEER_N_KEYS + i2[:rows, :])
            else:
                cand.append(v1[a0:a0 + n_a, :] + v2[0:1, :])
                cand_id.append(i1[a0:a0 + n_a, :] * PEER_N_KEYS + i2[0:1, :])
        top_s, top_e = _topk_rows(jnp.concatenate(cand, axis=0), flat, jnp.concatenate(cand_id, axis=0), PEER_TOPK)
        ex = jnp.exp(top_s - jnp.max(top_s, axis=0, keepdims=True))
        experts.append(top_e)
        gates.append(ex / jnp.sum(ex, axis=0, keepdims=True))
    e_ref[...] = jnp.concatenate(experts, axis=0).T.astype(jnp.int32)
    gate = jnp.concatenate(gates, axis=0)
    tb = g_ref.shape[2]
    for j in range(tt // tb):
        g_ref[j] = gate[:, j * tb:(j + 1) * tb]


def peer_route(qp, keys1, keys2, tb, tt=256):
    T, Q = qp.shape
    kspec = pl.BlockSpec(keys1.shape, lambda i: (0, 0))
    return pl.pallas_call(
        _peer_route_kernel,
        grid=(T // tt,),
        in_specs=[pl.BlockSpec((tt, Q), lambda i: (i, 0)), kspec, kspec],
        out_specs=[pl.BlockSpec((tt, PEER_SLOTS), lambda i: (i, 0)),
                   pl.BlockSpec((tt // tb, PEER_SLOTS, tb), lambda i: (i, 0, 0))],
        out_shape=[jax.ShapeDtypeStruct((T, PEER_SLOTS), jnp.int32),
                   jax.ShapeDtypeStruct((T // tb, PEER_SLOTS, tb), F32)],
        compiler_params=_params("parallel"),
        name="peer_route",
    )(qp, keys1, keys2)


X_ROWS = D_MODEL // LANE
UV_ROWS = 2 * X_ROWS
PEER_NSLOT = 8
PEER_TOKEN_BLOCK = 64
DMA_PRIORITIES = 2


def _pack_uv(u, v):
    n_exp = u.shape[0]
    return jnp.concatenate([u.reshape(n_exp, X_ROWS, LANE), v.reshape(n_exp, X_ROWS, LANE)], axis=1).astype(BF16)


def _peer_apply_kernel(idx_ref, idx_next_ref, h_ref, x1_ref, gn_ref, gt_ref, uv_hbm, o_ref, *scratch, tb, normalize):
    bufs = scratch[:PEER_NSLOT]
    p_ref, w_ref, xg_ref, og_ref, sem = scratch[PEER_NSLOT:]
    ahead = PEER_NSLOT - 1
    step = pl.program_id(0)

    def row_copy(ids, t, j, slot, row):
        return pltpu.make_async_copy(uv_hbm.at[ids[t, j]], bufs[slot].at[pl.ds(row, UV_ROWS), :], sem.at[slot])

    def issue(ids, t, slot):
        for j in range(PEER_SLOTS):
            row_copy(ids, t, j, slot, j * UV_ROWS).start(priority=j % DMA_PRIORITIES)

    def issue_rolled(ids, t, slot):
        def body(j, carry):
            row_copy(ids, t, j, slot, pl.multiple_of(j * UV_ROWS, UV_ROWS)).start()
            return carry
        lax.fori_loop(0, PEER_SLOTS, body, 0)

    def wait_all(slot):
        pltpu.make_async_copy(bufs[(slot + 1) % PEER_NSLOT], bufs[slot], sem.at[slot]).wait()

    lane = lax.broadcasted_iota(jnp.int32, (PEER_SLOTS, tb), 1)

    def compute(t, slot):
        buf = bufs[slot]
        x = jnp.concatenate([xg_ref[slot:slot + 1, c * LANE:(c + 1) * LANE] for c in range(X_ROWS)], axis=0)
        for j in range(PEER_SLOTS):
            prod = buf[j * UV_ROWS:j * UV_ROWS + X_ROWS, :].astype(F32) * x
            p_ref[j * SUBLANE:(j + 1) * SUBLANE, :] = prod[:SUBLANE] + prod[SUBLANE:]
        acc = p_ref[pl.ds(0, PEER_SLOTS, stride=SUBLANE), :]
        for q in range(1, SUBLANE):
            acc = acc + p_ref[pl.ds(q, PEER_SLOTS, stride=SUBLANE), :]
        act = _gelu(jnp.sum(acc, axis=1, keepdims=True))
        gate = jnp.sum(jnp.where(lane == t, gt_ref[...], 0.0), axis=1, keepdims=True)
        w_ref[...] = jnp.broadcast_to(gate * act, (PEER_SLOTS, LANE))
        n_acc = 4
        out = [jnp.zeros((X_ROWS, LANE), F32) for _ in range(n_acc)]
        for j in range(PEER_SLOTS):
            base = j * UV_ROWS + X_ROWS
            out[j % n_acc] = out[j % n_acc] + buf[base:base + X_ROWS, :].astype(F32) * w_ref[j:j + 1, :]
        y = (out[0] + out[1]) + (out[2] + out[3])
        for c in range(X_ROWS):
            og_ref[slot:slot + 1, c * LANE:(c + 1) * LANE] = y[c:c + 1, :]

    def load_group(t0):
        xg_ref[...] = h_ref[pl.ds(t0, PEER_NSLOT), :]

    def store_group(t0):
        z = x1_ref[pl.ds(t0, PEER_NSLOT), :] + og_ref[...]
        o_ref[pl.ds(t0, PEER_NSLOT), :] = _rms(z, gn_ref[...]) if normalize else z

    def group(g, carry):
        t0 = pl.multiple_of(g * PEER_NSLOT, PEER_NSLOT)
        load_group(t0)
        for i in range(PEER_NSLOT):
            wait_all(i)
            issue(idx_ref, t0 + i + ahead, (i + ahead) % PEER_NSLOT)
            compute(t0 + i, i)
        store_group(t0)
        return carry

    @pl.when(step == 0)
    def _():
        for t in range(ahead):
            issue_rolled(idx_ref, t, t)

    n_groups = tb // PEER_NSLOT
    lax.fori_loop(0, n_groups - 1, group, 0)
    t0 = (n_groups - 1) * PEER_NSLOT
    load_group(t0)
    for i in range(PEER_NSLOT):
        wait_all(i)
        if i + ahead < PEER_NSLOT:
            issue(idx_ref, t0 + i + ahead, (i + ahead) % PEER_NSLOT)
        else:
            @pl.when(step + 1 < pl.num_programs(0))
            def _(i=i):
                issue(idx_next_ref, i + ahead - PEER_NSLOT, (i + ahead) % PEER_NSLOT)
        compute(t0 + i, i)
    store_group(t0)


def peer_apply(experts, h, x1, gate_blocks, uv, final_gain=None):
    T, D = h.shape
    n_blocks, _, tb = gate_blocks.shape
    assert PEER_NSLOT == SUBLANE and tb % PEER_NSLOT == 0
    normalize = final_gain is not None
    gain = (final_gain if normalize else jnp.ones((D,), F32)).reshape(1, D)
    tok_spec = pl.BlockSpec((tb, D), lambda i: (i, 0))
    return pl.pallas_call(
        functools.partial(_peer_apply_kernel, tb=tb, normalize=normalize),
        grid=(n_blocks,),
        in_specs=[
            pl.BlockSpec((tb, PEER_SLOTS), lambda i: (i, 0), memory_space=pltpu.SMEM),
            pl.BlockSpec((tb, PEER_SLOTS), lambda i: (jnp.minimum(i + 1, n_blocks - 1), 0), memory_space=pltpu.SMEM),
            tok_spec,
            tok_spec,
            pl.BlockSpec((1, D), lambda i: (0, 0)),
            pl.BlockSpec((None, PEER_SLOTS, tb), lambda i: (i, 0, 0)),
            pl.BlockSpec(memory_space=pl.ANY),
        ],
        out_specs=tok_spec,
        out_shape=jax.ShapeDtypeStruct((T, D), F32),
        scratch_shapes=[pltpu.VMEM((PEER_SLOTS * UV_ROWS, LANE), BF16) for _ in range(PEER_NSLOT)] + [
            pltpu.VMEM((PEER_SLOTS * SUBLANE, LANE), F32), pltpu.VMEM((PEER_SLOTS, LANE), F32),
            pltpu.VMEM((PEER_NSLOT, D), F32), pltpu.VMEM((PEER_NSLOT, D), F32),
            pltpu.SemaphoreType.DMA((PEER_NSLOT,))],
        compiler_params=_params("arbitrary"),
        name="peer_apply",
    )(experts, experts, h, x1, gain, gate_blocks, uv)


def _pack_w_in(w):
    u, v, q, kv, gate, merge = jnp.split(w, [2048, 4096, 6144, 9216, 9264], axis=1)
    gate = gate.reshape(D_MODEL, NSA_KV_HEADS, NSA_GROUP * 3)
    gate = jnp.pad(gate, ((0, 0), (0, 0), (0, GATE_PAD - NSA_GROUP * 3))).reshape(D_MODEL, NSA_KV_HEADS * GATE_PAD)
    part_a = jnp.concatenate([u, v, q, merge], axis=1).astype(BF16)
    part_b = jnp.concatenate([kv, gate], axis=1).astype(BF16)
    return part_a, part_b


def _rope_tables(S):
    half = HEAD_DIM // 2
    inv = ROPE_THETA ** (-jnp.arange(half, dtype=F32) / half)
    ang = jnp.arange(S, dtype=F32)[:, None] * inv[None, :]
    cos = jnp.cos(ang)
    sin = jnp.sin(ang)
    return jnp.concatenate([cos, cos], axis=1), jnp.concatenate([-sin, sin], axis=1)


def _layer(x, norm_mix_g, w_in, w_out, gm_ln_g, gm_ln_b, gm_spatial_w, gm_spatial_b,
           cmp_pos_k, cmp_w1_k, cmp_w2_k, cmp_pos_v, cmp_w1_v, cmp_w2_v,
           norm_ffn_g, peer_w_q, peer_keys1, peer_keys2, peer_u, peer_v, final_gain):
    B, S, D = x.shape
    T = B * S
    xt = x.reshape(T, D)
    h = rmsnorm_cast(xt, norm_mix_g)
    w_a, w_b = _pack_w_in(w_in)
    proj = matmul(h, w_a, 1024, 2048, BF16, "in_proj_a")
    proj_b = matmul(h, w_b, 1024, COL_B_TOTAL // 2, F32, "in_proj_b")
    o_a = gmlp_mixer(proj, gm_ln_g, gm_ln_b, gm_spatial_w, gm_spatial_b)
    proj3 = proj.reshape(B, S, COL_A_TOTAL)
    proj_b3 = proj_b.reshape(B, S, COL_B_TOTAL)
    cos, sin = _rope_tables(S)
    kc, vc, ks, vs, kw, vw = nsa_prep(proj_b3, cos, sin, cmp_pos_k, cmp_w1_k, cmp_w2_k, cmp_pos_v, cmp_w1_v, cmp_w2_v)
    o_b = nsa_attention(proj3, proj_b3, kc, vc, ks, vs, kw, vw, cos, sin).reshape(T, D)
    x1, h2, h2b = merge_out(proj, o_a, o_b, xt, w_out.astype(BF16), norm_ffn_g)
    qp = matmul(h2b, peer_w_q.astype(BF16), 1024, 1024, F32, "peer_query")
    experts, gate_blocks = peer_route(qp, peer_keys1, peer_keys2, PEER_TOKEN_BLOCK)
    out = peer_apply(experts, h2, x1, gate_blocks, _pack_uv(peer_u, peer_v), final_gain)
    return out.reshape(B, S, D)


def kernel(x, norm_mix_g, w_in, w_out, gm_ln_g, gm_ln_b, gm_spatial_w, gm_spatial_b, cmp_pos_k, cmp_w1_k, cmp_w2_k, cmp_pos_v, cmp_w1_v, cmp_w2_v, norm_ffn_g, peer_w_q, peer_keys1, peer_keys2, peer_u, peer_v, norm_final_g):
    depth = w_in.shape[0]
    for l in range(depth):
        x = _layer(x, norm_mix_g[l], w_in[l], w_out[l], gm_ln_g[l], gm_ln_b[l], gm_spatial_w[l], gm_spatial_b[l],
                   cmp_pos_k[l], cmp_w1_k[l], cmp_w2_k[l], cmp_pos_v[l], cmp_w1_v[l], cmp_w2_v[l],
                   norm_ffn_g[l], peer_w_q[l], peer_keys1[l], peer_keys2[l], peer_u[l], peer_v[l],
                   norm_final_g if l == depth - 1 else None)
    return x
```
